```python
import jax, jax.numpy as jnp
from jax import lax
import numpy as np

D_MODEL = 4096
BATCH = 2
SEQ = 8192
DEPTH = 4
DEC_BATCH = 8
DEC_SEQ = 16
PAST_LEN = 4096

CHUNK = 64
N_MIXERS = 2
N_RWKV_LAYERS = (DEPTH + N_MIXERS - 1) // N_MIXERS
N_ATTN_LAYERS = DEPTH // N_MIXERS
BAND_CHUNKS = 8
BAND_PAST = BAND_CHUNKS * CHUNK
BAND = BAND_PAST + CHUNK
ATTN_HEAD_DIM = 128
ATTN_HEADS = D_MODEL // ATTN_HEAD_DIM
REL_MAX = 256
N_REL = REL_MAX + CHUNK
RWKV_HEAD_SIZE = 64
RWKV_HEADS = D_MODEL // RWKV_HEAD_SIZE
LORA_DECAY = max(32, int(round(1.8 * D_MODEL ** 0.5 / 32)) * 32)
LORA_AAA = max(32, int(round(1.8 * D_MODEL ** 0.5 / 32)) * 32)
LORA_MV = max(32, int(round(1.3 * D_MODEL ** 0.5 / 32)) * 32)
LORA_GATE = max(32, int(round(0.6 * D_MODEL ** 0.8 / 32)) * 32)
D_FF = int(round(8 * D_MODEL / 3 / 256)) * 256
PLE_DIM = 256
ALPHA = (2 * DEPTH) ** 0.25
BETA = (8 * DEPTH) ** -0.25
LN_EPS = 1e-5
GN_EPS = 64e-5
NEG_INF = -1e30

kernel_name = 'rwkv7_chunkband_macaron_deepnorm_step'


def _layer_norm(x, g, b):
    xf = x.astype(jnp.float32)
    mu = jnp.mean(xf, axis=-1, keepdims=True)
    var = jnp.mean(jnp.square(xf - mu), axis=-1, keepdims=True)
    return ((xf - mu) * lax.rsqrt(var + LN_EPS) * g + b).astype(x.dtype)


def _swiglu(x, w_gate, w_up, w_down):
    return (jax.nn.silu(x @ w_gate) * (x @ w_up)) @ w_down


def _wkv7_step(S, inp):
    r, w, k, v, a, b = inp
    sa = jnp.einsum('bhvk,bhk->bhv', S, a)
    S = S * w[:, :, None, :] + sa[..., None] * b[:, :, None, :] + v[..., None] * k[:, :, None, :]
    y = jnp.einsum('bhvk,bhk->bhv', S, r)
    return S, y


def _rwkv7_mix(x, x_prev, S0, v_first, j, W):
    B, T, _ = x.shape
    f32 = jnp.float32
    xx = jnp.concatenate([x_prev[:, None, :], x[:, :-1]], axis=1) - x
    mu = W['rwkv_mu'][j]
    xr, xw, xk, xv, xa, xg = [x + xx * mu[m] for m in range(6)]
    r = xr @ W['rwkv_w_r'][j]
    w_log = -jax.nn.softplus(-(W['rwkv_w0'][j] + jnp.tanh(xw @ W['rwkv_w1'][j]) @ W['rwkv_w2'][j])) - 0.5
    k = xk @ W['rwkv_w_k'][j]
    v = xv @ W['rwkv_w_v'][j]
    if v_first is None:
        v_first = v
    else:
        jv = j - 1
        v = v + (v_first - v) * jax.nn.sigmoid(W['rwkv_v0'][jv] + (xv @ W['rwkv_v1'][jv]) @ W['rwkv_v2'][jv])
    a = jax.nn.sigmoid(W['rwkv_a0'][j] + (xa @ W['rwkv_a1'][j]) @ W['rwkv_a2'][j])
    g = jax.nn.sigmoid(xg @ W['rwkv_g1'][j]) @ W['rwkv_g2'][j]

    def heads(t):
        return t.reshape(B, T, RWKV_HEADS, RWKV_HEAD_SIZE).astype(f32)

    kk = heads(k * W['rwkv_k_k'][j])
    kk = kk / jnp.maximum(jnp.sqrt(jnp.sum(kk * kk, axis=-1, keepdims=True)), 1e-12)
    k = k * (1.0 + (a - 1.0) * W['rwkv_k_a'][j])
    decay = jnp.exp(-jnp.exp(heads(w_log)))
    r_h, k_h, v_h, a_h = heads(r), heads(k), heads(v), heads(a)
    seq = tuple(jnp.moveaxis(t, 1, 0) for t in (r_h, decay, k_h, v_h, -kk, kk * a_h))
    S, y = lax.scan(_wkv7_step, S0.astype(f32), seq)
    y = jnp.moveaxis(y, 0, 1)
    m = jnp.mean(y, axis=-1, keepdims=True)
    var = jnp.mean(jnp.square(y - m), axis=-1, keepdims=True)
    ln_w = W['rwkv_ln_w'][j].reshape(RWKV_HEADS, RWKV_HEAD_SIZE)
    ln_b = W['rwkv_ln_b'][j].reshape(RWKV_HEADS, RWKV_HEAD_SIZE)
    y = (y - m) * lax.rsqrt(var + GN_EPS) * ln_w + ln_b
    bonus = jnp.sum(r_h * k_h * W['rwkv_r_k'][j], axis=-1, keepdims=True) * v_h
    out = ((y + bonus).reshape(B, T, D_MODEL).astype(x.dtype) * g) @ W['rwkv_w_o'][j]
    return out, x[:, -1], S.astype(x.dtype), v_first


def _rel_bias(table, diff):
    idx = jnp.clip(diff, -(CHUNK - 1), REL_MAX) + (CHUNK - 1)
    return table[:, idx].astype(jnp.float32)


def _band_attend(q, k, v, bias, mask):
    s = jnp.einsum('bqhd,bkhd->bhqk', q, k).astype(jnp.float32) * (ATTN_HEAD_DIM ** -0.5) + bias[None]
    if mask is not None:
        s = jnp.where(mask[None, None], s, NEG_INF)
    pr = jax.nn.softmax(s, axis=-1).astype(v.dtype)
    return jnp.einsum('bhqk,bkhd->bqhd', pr, v)


def _qkv(x, w_qkv):
    B, T, _ = x.shape
    q, k, v = jnp.split(x @ w_qkv, 3, axis=-1)
    shp = (B, T, ATTN_HEADS, ATTN_HEAD_DIM)
    return q.reshape(shp), k.reshape(shp), v.reshape(shp)


def _band_attn_prompt(x, w_qkv, w_o, rel_table):
    B, T, _ = x.shape
    n_chunks = T // CHUNK
    q, k, v = _qkv(x, w_qkv)
    pad = jnp.zeros((B, BAND_PAST, ATTN_HEADS, ATTN_HEAD_DIM), k.dtype)
    kp = jnp.concatenate([pad, k], axis=1)
    vp = jnp.concatenate([pad, v], axis=1)
    qi = jnp.arange(CHUNK)
    kj = jnp.arange(BAND)
    bias = _rel_bias(rel_table, qi[:, None] + BAND_PAST - kj[None, :])
    qc = jnp.moveaxis(q.reshape(B, n_chunks, CHUNK, ATTN_HEADS, ATTN_HEAD_DIM), 1, 0)

    def one_chunk(args):
        c, q_blk = args
        start = c * CHUNK
        k_blk = lax.dynamic_slice_in_dim(kp, start, BAND, axis=1)
        v_blk = lax.dynamic_slice_in_dim(vp, start, BAND, axis=1)
        valid = (kj >= BAND_PAST - start)[None, :]
        return _band_attend(q_blk, k_blk, v_blk, bias, valid)

    o = lax.map(one_chunk, (jnp.arange(n_chunks), qc))
    o = jnp.moveaxis(o, 0, 1).reshape(B, T, D_MODEL)
    rows = min(BAND_PAST, T)
    return o @ w_o, k[:, T - rows:], v[:, T - rows:]


def _band_attn_sample(x, k_cache, v_cache, w_qkv, w_o, rel_table):
    B, T, _ = x.shape
    rows = k_cache.shape[1]
    q, k, v = _qkv(x, w_qkv)
    kf = jnp.concatenate([k_cache.astype(k.dtype), k], axis=1)
    vf = jnp.concatenate([v_cache.astype(v.dtype), v], axis=1)
    q_loc = jnp.arange(T)
    k_loc = jnp.arange(rows + T) - rows
    bias = _rel_bias(rel_table, q_loc[:, None] - k_loc[None, :])
    o = _band_attend(q, kf, vf, bias, None).reshape(B, T, D_MODEL)
    return o @ w_o, k, v


def _trunk(x, p, shift0, wkv0, cache_k, cache_v, W, is_prompt):
    v_first = None
    shifts, wkvs, ks, vs = [], [], [], []
    for i in range(DEPTH):
        h = _swiglu(x, W['ffn_w_gate'][i, 0], W['ffn_w_up'][i, 0], W['ffn_w_down'][i, 0])
        x = _layer_norm(ALPHA * x + 0.5 * h, W['ln_g'][i, 0], W['ln_b'][i, 0])
        j = i // N_MIXERS
        if i % N_MIXERS == 0:
            h, last, S, v_first = _rwkv7_mix(x, shift0[:, j], wkv0[:, j], v_first, j, W)
            shifts.append(last)
            wkvs.append(S)
        elif is_prompt:
            h, k_new, v_new = _band_attn_prompt(x, W['attn_w_qkv'][j], W['attn_w_o'][j], W['attn_rel_bias'][j])
            ks.append(k_new)
            vs.append(v_new)
        else:
            h, k_new, v_new = _band_attn_sample(x, cache_k[:, j], cache_v[:, j], W['attn_w_qkv'][j],
                                                W['attn_w_o'][j], W['attn_rel_bias'][j])
            ks.append(k_new)
            vs.append(v_new)
        x = _layer_norm(ALPHA * x + h, W['ln_g'][i, 1], W['ln_b'][i, 1])
        h = _swiglu(x, W['ffn_w_gate'][i, 1], W['ffn_w_up'][i, 1], W['ffn_w_down'][i, 1])
        x = _layer_norm(ALPHA * x + 0.5 * h, W['ln_g'][i, 2], W['ln_b'][i, 2])
        x = x + jax.nn.sigmoid(x @ W['ple_w_gate'][i]) * (p[i].astype(x.dtype) @ W['ple_w_proj'][i])
    return x, jnp.stack(shifts, axis=1), jnp.stack(wkvs, axis=1), jnp.stack(ks, axis=1), jnp.stack(vs, axis=1)


def setup_inputs(seed: int = 0) -> dict:
    key = jax.random.key(seed)
    keys = iter(jax.random.split(key, 64))
    f32 = jnp.float32

    def nrm(shape, scale=1.0):
        return jax.random.normal(next(keys), shape, f32) * scale

    def uni(shape, lo, hi):
        return jax.random.uniform(next(keys), shape, f32, lo, hi)

    D, L, A, Bn = D_MODEL, DEPTH, N_RWKV_LAYERS, N_ATTN_LAYERS
    H, N = RWKV_HEADS, RWKV_HEAD_SIZE
    rows = min(BAND_PAST, PAST_LEN)
    s_in = D ** -0.5
    return {
        'x_prompt': nrm((BATCH, SEQ, D)),
        'x_sample': nrm((DEC_BATCH, DEC_SEQ, D)),
        'state_shift': nrm((DEC_BATCH, A, D)),
        'state_wkv': nrm((DEC_BATCH, A, H, N, N), 0.3),
        'cache_k': nrm((DEC_BATCH, Bn, rows, ATTN_HEADS, ATTN_HEAD_DIM)),
        'cache_v': nrm((DEC_BATCH, Bn, rows, ATTN_HEADS, ATTN_HEAD_DIM)),
        'p_prompt': nrm((L, BATCH, SEQ, PLE_DIM)),
        'p_sample': nrm((L, DEC_BATCH, DEC_SEQ, PLE_DIM)),
        'ln_g': 1.0 + nrm((L, 3, D), 0.05),
        'ln_b': nrm((L, 3, D), 0.02),
        'ffn_w_gate': nrm((L, 2, D, D_FF), s_in),
        'ffn_w_up': nrm((L, 2, D, D_FF), s_in),
        'ffn_w_down': nrm((L, 2, D_FF, D), D_FF ** -0.5 * BETA),
        'ple_w_gate': nrm((L, D, D), s_in),
        'ple_w_proj': nrm((L, PLE_DIM, D), PLE_DIM ** -0.5),
        'rwkv_mu': uni((A, 6, D), 0.0, 1.0),
        'rwkv_w_r': nrm((A, D, D), s_in),
        'rwkv_w_k': nrm((A, D, D), s_in),
        'rwkv_w_v': nrm((A, D, D), s_in),
        'rwkv_w_o': nrm((A, D, D), s_in * BETA),
        'rwkv_w0': uni((A, D), -5.0, -1.0),
        'rwkv_w1': nrm((A, D, LORA_DECAY), s_in),
        'rwkv_w2': nrm((A, LORA_DECAY, D), LORA_DECAY ** -0.5 * 0.5),
        'rwkv_a0': nrm((A, D), 0.5),
        'rwkv_a1': nrm((A, D, LORA_AAA), s_in),
        'rwkv_a2': nrm((A, LORA_AAA, D), LORA_AAA ** -0.5),
        'rwkv_v0': nrm((A - 1, D), 0.5),
        'rwkv_v1': nrm((A - 1, D, LORA_MV), s_in),
        'rwkv_v2': nrm((A - 1, LORA_MV, D), LORA_MV ** -0.5),
        'rwkv_g1': nrm((A, D, LORA_GATE), s_in),
        'rwkv_g2': nrm((A, LORA_GATE, D), LORA_GATE ** -0.5),
        'rwkv_k_k': 0.85 + nrm((A, D), 0.05),
        'rwkv_k_a': 1.0 + nrm((A, D), 0.05),
        'rwkv_r_k': nrm((A, H, N), 0.1),
        'rwkv_ln_w': 1.0 + nrm((A, D), 0.05),
        'rwkv_ln_b': nrm((A, D), 0.02),
        'attn_w_qkv': nrm((Bn, D, 3 * D), s_in),
        'attn_w_o': nrm((Bn, D, D), s_in * BETA),
        'attn_rel_bias': nrm((Bn, ATTN_HEADS, N_REL), 0.5),
    }


def reference(x_prompt, x_sample, state_shift, state_wkv, cache_k, cache_v, p_prompt, p_sample,
              ln_g, ln_b, ffn_w_gate, ffn_w_up, ffn_w_down, ple_w_gate, ple_w_proj,
              rwkv_mu, rwkv_w_r, rwkv_w_k, rwkv_w_v, rwkv_w_o, rwkv_w0, rwkv_w1, rwkv_w2,
              rwkv_a0, rwkv_a1, rwkv_a2, rwkv_v0, rwkv_v1, rwkv_v2, rwkv_g1, rwkv_g2,
              rwkv_k_k, rwkv_k_a, rwkv_r_k, rwkv_ln_w, rwkv_ln_b,
              attn_w_qkv, attn_w_o, attn_rel_bias):
    W = dict(ln_g=ln_g, ln_b=ln_b, ffn_w_gate=ffn_w_gate, ffn_w_up=ffn_w_up, ffn_w_down=ffn_w_down,
             ple_w_gate=ple_w_gate, ple_w_proj=ple_w_proj,
             rwkv_mu=rwkv_mu, rwkv_w_r=rwkv_w_r, rwkv_w_k=rwkv_w_k, rwkv_w_v=rwkv_w_v, rwkv_w_o=rwkv_w_o,
             rwkv_w0=rwkv_w0, rwkv_w1=rwkv_w1, rwkv_w2=rwkv_w2,
             rwkv_a0=rwkv_a0, rwkv_a1=rwkv_a1, rwkv_a2=rwkv_a2,
             rwkv_v0=rwkv_v0, rwkv_v1=rwkv_v1, rwkv_v2=rwkv_v2,
             rwkv_g1=rwkv_g1, rwkv_g2=rwkv_g2, rwkv_k_k=rwkv_k_k, rwkv_k_a=rwkv_k_a, rwkv_r_k=rwkv_r_k,
             rwkv_ln_w=rwkv_ln_w, rwkv_ln_b=rwkv_ln_b,
             attn_w_qkv=attn_w_qkv, attn_w_o=attn_w_o, attn_rel_bias=attn_rel_bias)
    bp = x_prompt.shape[0]
    zero_shift = jnp.zeros((bp, N_RWKV_LAYERS, D_MODEL), x_prompt.dtype)
    zero_wkv = jnp.zeros((bp, N_RWKV_LAYERS, RWKV_HEADS, RWKV_HEAD_SIZE, RWKV_HEAD_SIZE), x_prompt.dtype)
    y_prompt, shift_p, wkv_p, k_p, v_p = _trunk(x_prompt, p_prompt, zero_shift, zero_wkv, None, None, W, True)
    y_sample, shift_s, wkv_s, k_s, v_s = _trunk(x_sample, p_sample, state_shift, state_wkv, cache_k, cache_v, W, False)
    return (y_prompt, y_sample, shift_p, wkv_p, k_p, v_p, shift_s, wkv_s, k_s, v_s)
```

```python
import functools
import math

import jax
import jax.numpy as jnp
from jax import lax
from jax.experimental import pallas as pl
from jax.experimental.pallas import tpu as pltpu

F32 = jnp.float32
BF16 = jnp.bfloat16

CHUNK = 64
BAND_CHUNKS = 8
BAND_PAST = BAND_CHUNKS * CHUNK
REL_MAX = 256
ATTN_HEAD_DIM = 128
RWKV_HEAD = 64
LN_EPS = 1e-5
GN_EPS = 64e-5
NEG_INF = -1e30

LANES = 128
SLAB = 256
HEADS_PER_SLAB = SLAB // RWKV_HEAD
WKV_CHUNK = 64
KEY_WIN = 640
FF_ALIGN = 512
VMEM_LIMIT = 56 * 1024 * 1024


def _cparams(sem):
    return pltpu.CompilerParams(dimension_semantics=sem, vmem_limit_bytes=VMEM_LIMIT)


def _pow2_tile(m, cap):
    if m <= cap:
        return m
    t = 1 << (cap.bit_length() - 1)
    while m % t:
        t //= 2
    return t


def _lane_tile(n, cap):
    t = max(LANES, (cap // LANES) * LANES)
    while n % t:
        t -= LANES
    return t


def _dot(a, b):
    return jnp.dot(a.astype(BF16), b.astype(BF16), preferred_element_type=F32)


def _dot_nt(a, b):
    return lax.dot_general(a.astype(BF16), b.astype(BF16), (((1,), (1,)), ((), ())),
                           preferred_element_type=F32)


def _dot_tn(a, b):
    return lax.dot_general(a.astype(BF16), b.astype(BF16), (((0,), (0,)), ((), ())),
                           preferred_element_type=F32)


def _mm_kernel(x_ref, w_ref, o_ref):
    o_ref[...] = jnp.dot(x_ref[...], w_ref[...], preferred_element_type=F32).astype(o_ref.dtype)


def _mm_tiles(m, k, n):
    tm = _pow2_tile(m, max(8, (12 << 20) // (2 * k)))
    tn = _lane_tile(n, max(LANES, (6 << 20) // (2 * k)))
    return tm, tn


def _matmul(x, w, out_dtype, name):
    m, k = x.shape
    n = w.shape[1]
    tm, tn = _mm_tiles(m, k, n)
    return pl.pallas_call(
        _mm_kernel,
        grid=(m // tm, n // tn),
        in_specs=[pl.BlockSpec((tm, k), lambda i, j: (i, 0)),
                  pl.BlockSpec((k, tn), lambda i, j: (0, j))],
        out_specs=pl.BlockSpec((tm, tn), lambda i, j: (i, j)),
        out_shape=jax.ShapeDtypeStruct((m, n), out_dtype),
        compiler_params=_cparams(("parallel", "arbitrary")),
        name=name,
    )(x, w)


def _ffn_up_kernel(x_ref, wg_ref, wu_ref, o_ref):
    x = x_ref[...]
    g = jnp.dot(x, wg_ref[...], preferred_element_type=F32)
    u = jnp.dot(x, wu_ref[...], preferred_element_type=F32)
    o_ref[...] = (g * jax.nn.sigmoid(g) * u).astype(o_ref.dtype)


def _ffn_up(xb, wg, wu):
    m, k = xb.shape
    n = wg.shape[1]
    tm = _pow2_tile(m, max(8, (8 << 20) // (2 * k)))
    tn = _lane_tile(n, max(LANES, (4 << 20) // (2 * k)))
    return pl.pallas_call(
        _ffn_up_kernel,
        grid=(m // tm, n // tn),
        in_specs=[pl.BlockSpec((tm, k), lambda i, j: (i, 0)),
                  pl.BlockSpec((k, tn), lambda i, j: (0, j)),
                  pl.BlockSpec((k, tn), lambda i, j: (0, j))],
        out_specs=pl.BlockSpec((tm, tn), lambda i, j: (i, j)),
        out_shape=jax.ShapeDtypeStruct((m, n), BF16),
        compiler_params=_cparams(("parallel", "arbitrary")),
        name="ffn_up",
    )(xb, wg, wu)


def _add_ln_kernel(x_ref, y_ref, g_ref, b_ref, of_ref, ob_ref, *, alpha, scale):
    z = alpha * x_ref[...] + scale * y_ref[...]
    mu = jnp.mean(z, axis=-1, keepdims=True)
    zc = z - mu
    var = jnp.mean(zc * zc, axis=-1, keepdims=True)
    o = zc * lax.rsqrt(var + LN_EPS) * g_ref[...] + b_ref[...]
    of_ref[...] = o
    ob_ref[...] = o.astype(BF16)


def _add_ln(x, y, g, b, alpha, scale):
    m, d = x.shape
    tr = _pow2_tile(m, max(8, (4 << 20) // (4 * d)))
    row = pl.BlockSpec((tr, d), lambda i: (i, 0))
    vec = pl.BlockSpec((1, d), lambda i: (0, 0))
    return pl.pallas_call(
        functools.partial(_add_ln_kernel, alpha=alpha, scale=scale),
        grid=(m // tr,),
        in_specs=[row, row, vec, vec],
        out_specs=[row, row],
        out_shape=[jax.ShapeDtypeStruct((m, d), F32), jax.ShapeDtypeStruct((m, d), BF16)],
        compiler_params=_cparams(("parallel",)),
        name="add_ln",
    )(x, y, g.reshape(1, d), b.reshape(1, d))


def _ple_kernel(xb_ref, wg_ref, p_ref, wp_ref, x_ref, of_ref, ob_ref):
    gate = jax.nn.sigmoid(jnp.dot(xb_ref[...], wg_ref[...], preferred_element_type=F32))
    emb = jnp.dot(p_ref[...].astype(BF16), wp_ref[...], preferred_element_type=F32)
    o = x_ref[...] + gate * emb
    of_ref[...] = o
    ob_ref[...] = o.astype(BF16)


def _ple(x, xb, p, wg, wp):
    m, d = x.shape
    pd = p.shape[1]
    tm = _pow2_tile(m, max(8, (8 << 20) // (2 * d)))
    tn = _lane_tile(d, max(LANES, (4 << 20) // (2 * d)))
    blk = pl.BlockSpec((tm, tn), lambda i, j: (i, j))
    return pl.pallas_call(
        _ple_kernel,
        grid=(m // tm, d // tn),
        in_specs=[pl.BlockSpec((tm, d), lambda i, j: (i, 0)),
                  pl.BlockSpec((d, tn), lambda i, j: (0, j)),
                  pl.BlockSpec((tm, pd), lambda i, j: (i, 0)),
                  pl.BlockSpec((pd, tn), lambda i, j: (0, j)),
                  blk],
        out_specs=[blk, blk],
        out_shape=[jax.ShapeDtypeStruct((m, d), F32), jax.ShapeDtypeStruct((m, d), BF16)],
        compiler_params=_cparams(("parallel", "arbitrary")),
        name="ple",
    )(xb, wg, p, wp, x)


def _mix_kernel(x_ref, prev_ref, mu_ref, *o_refs):
    x = x_ref[...]
    shifted = pltpu.roll(x, 1, 0)
    first = lax.broadcasted_iota(jnp.int32, x.shape, 0) == 0
    shifted = jnp.where(first, prev_ref[...], shifted)
    xx = shifted - x
    for m, o_ref in enumerate(o_refs):
        o_ref[...] = (x + xx * mu_ref[m:m + 1, :]).astype(BF16)


def _token_mixes(x, x_prev, mu):
    b, t, d = x.shape
    tt = _pow2_tile(t, max(8, (4 << 20) // (4 * d)))
    nt = t // tt
    prev = jnp.concatenate([x_prev[:, None, :], x[:, tt - 1:t - 1:tt, :]], axis=1).reshape(b, nt, 1, d)
    blk = pl.BlockSpec((None, tt, d), lambda i, j: (i, j, 0))
    outs = pl.pallas_call(
        _mix_kernel,
        grid=(b, nt),
        in_specs=[blk,
                  pl.BlockSpec((None, None, 1, d), lambda i, j: (i, j, 0, 0)),
                  pl.BlockSpec((6, d), lambda i, j: (0, 0))],
        out_specs=[blk] * 6,
        out_shape=[jax.ShapeDtypeStruct((b, t, d), BF16)] * 6,
        compiler_params=_cparams(("parallel", "parallel")),
        name="token_mix",
    )(x, prev, mu)
    return [o.reshape(b * t, d) for o in outs]


def _lora_kernel(x_ref, w1_ref, w2_ref, bias_ref, *rest, mode):
    t = jnp.dot(x_ref[...], w1_ref[...], preferred_element_type=F32)
    if mode == "decay":
        t = jnp.tanh(t)
    elif mode == "gate":
        t = jax.nn.sigmoid(t)
    z = jnp.dot(t.astype(BF16), w2_ref[...], preferred_element_type=F32)
    if mode == "decay":
        u = -(bias_ref[...] + z)
        softplus = jnp.maximum(u, 0.0) + jnp.log(1.0 + jnp.exp(-jnp.abs(u)))
        rest[0][...] = jnp.exp(-softplus - 0.5)
    elif mode == "gate":
        rest[0][...] = z
    elif mode == "sigmoid":
        rest[0][...] = jax.nn.sigmoid(bias_ref[...] + z)
    else:
        v_ref, vf_ref, o_ref = rest
        v = v_ref[...]
        o_ref[...] = v + (vf_ref[...] - v) * jax.nn.sigmoid(bias_ref[...] + z)


def _lora(xb, w1, w2, bias, mode, extra=()):
    m, d = xb.shape
    r = w1.shape[1]
    tm = _pow2_tile(m, 256)
    row = pl.BlockSpec((tm, d), lambda i: (i, 0))
    return pl.pallas_call(
        functools.partial(_lora_kernel, mode=mode),
        grid=(m // tm,),
        in_specs=[row,
                  pl.BlockSpec((d, r), lambda i: (0, 0)),
                  pl.BlockSpec((r, d), lambda i: (0, 0)),
                  pl.BlockSpec((1, d), lambda i: (0, 0))] + [row] * len(extra),
        out_specs=row,
        out_shape=jax.ShapeDtypeStruct((m, d), F32),
        compiler_params=_cparams(("parallel",)),
        name="lora_" + mode,
    )(xb, w1, w2, bias.reshape(1, d), *extra)


def _pad_lora(w1, w2):
    r = w1.shape[1]
    rp = -(-r // LANES) * LANES
    w1 = jnp.pad(w1.astype(BF16), ((0, 0), (0, rp - r)))
    w2 = jnp.pad(w2.astype(BF16), ((0, rp - r), (0, 0)))
    return w1, w2


def _split_bf16(x):
    hi = x.astype(BF16)
    lo = (x - hi.astype(F32)).astype(BF16)
    return hi, lo


def _wkv_kernel(r_ref, e_ref, k_ref, v_ref, a_ref, g_ref, prm_ref, s0_ref, o_ref, sf_ref, s_scr,
                *, chunk, n_chunks):
    c = chunk
    rows = HEADS_PER_SLAB * c
    tstep = pl.program_id(2)

    lane = lax.broadcasted_iota(jnp.int32, (1, SLAB), 1)
    head_masks = [(lane // RWKV_HEAD) == j for j in range(HEADS_PER_SLAB)]
    ri = lax.broadcasted_iota(jnp.int32, (rows, rows), 0)
    ci = lax.broadcasted_iota(jnp.int32, (rows, rows), 1)
    tri_strict = (ri % c) > (ci % c)
    tri_incl = (ri % c) >= (ci % c)
    eye = jnp.where(ri == ci, 1.0, 0.0).astype(F32)
    rs = lax.broadcasted_iota(jnp.int32, (SLAB, SLAB), 0)
    cs = lax.broadcasted_iota(jnp.int32, (SLAB, SLAB), 1)
    same_head = (rs // RWKV_HEAD) == (cs // RWKV_HEAD)
    ones_bd = jnp.where(same_head, 1.0, 0.0).astype(BF16)
    lt_incl = jnp.where(lax.broadcasted_iota(jnp.int32, (c, c), 0) >= lax.broadcasted_iota(jnp.int32, (c, c), 1),
                        1.0, 0.0).astype(BF16)

    @pl.when(tstep == 0)
    def _():
        s0 = s0_ref[...]
        s_scr[...] = jnp.where(same_head, jnp.concatenate([s0] * HEADS_PER_SLAB, axis=0), 0.0)

    prm = prm_ref[...]
    k_k, k_a, r_k, ln_w, ln_b = (prm[i:i + 1, :] for i in range(5))

    def stack(x):
        return jnp.concatenate([jnp.where(m, x, 0.0) for m in head_masks], axis=0)

    def fold(y):
        out = y[0:c]
        for j in range(1, HEADS_PER_SLAB):
            out = out + y[j * c:(j + 1) * c]
        return out

    def head_sums(xs):
        parts = []
        for x in xs:
            parts.extend(_split_bf16(x))
        tot = jnp.dot(jnp.concatenate(parts, axis=0), ones_bd, preferred_element_type=F32)
        return [tot[2 * i * c:(2 * i + 1) * c] + tot[(2 * i + 1) * c:(2 * i + 2) * c] for i in range(len(xs))]

    n_sq = int(math.log2(c)) - 1

    def body(idx, carry):
        sl = pl.ds(pl.multiple_of(idx * c, c), c)
        r = r_ref[sl, :]
        e = e_ref[sl, :]
        k = k_ref[sl, :]
        v = v_ref[sl, :]
        a = a_ref[sl, :]

        kk = k * k_k
        k2 = k * (1.0 + (a - 1.0) * k_a)
        ss, bonus = head_sums([kk * kk, r * k2 * r_k])
        kkn = kk / jnp.maximum(jnp.sqrt(ss), 1e-12)
        av = -kkn
        bv = kkn * a

        e_hi, e_lo = _split_bf16(e)
        cum = (jnp.dot(lt_incl, e_hi, preferred_element_type=F32)
               + jnp.dot(lt_incl, e_lo, preferred_element_type=F32))
        cum_last = cum[c - 1:c, :]
        p_in = jnp.exp(-cum)
        p_ex = jnp.exp(e - cum)
        p_inv = jnp.exp(cum)
        p_end = jnp.exp(cum - cum_last)
        p_c = jnp.exp(-cum_last)

        a_st = stack(av * p_ex)
        r_st = stack(r * p_in)
        b_st = stack(bv * p_inv)
        k_st = stack(k2 * p_inv)
        v_st = stack(v)
        ar = jnp.concatenate([a_st, r_st], axis=0).astype(BF16)
        xb = _dot_nt(ar, b_st)
        xk = _dot_nt(ar, k_st)
        a_ab = jnp.where(tri_strict, xb[:rows], 0.0)
        a_rb = jnp.where(tri_incl, xb[rows:], 0.0)
        a_ak = jnp.where(tri_strict, xk[:rows], 0.0)
        a_rk = jnp.where(tri_incl, xk[rows:], 0.0)

        x_pow = a_ab
        t_inv = eye + a_ab
        for _ in range(n_sq):
            x_pow = _dot(x_pow, x_pow)
            t_inv = t_inv + _dot(t_inv, x_pow)

        s_bd = s_scr[...]
        ars = _dot_nt(ar, s_bd)
        u_st = _dot(t_inv, ars[:rows] + _dot(a_ak, v_st))
        y = fold(ars[rows:] + _dot(a_rb, u_st) + _dot(a_rk, v_st))
        s_scr[...] = s_bd * p_c + _dot_tn(u_st, stack(bv * p_end)) + _dot_tn(v_st, stack(k2 * p_end))

        (mean,) = head_sums([y])
        yc = y - mean * (1.0 / RWKV_HEAD)
        (var,) = head_sums([yc * yc])
        yn = yc * lax.rsqrt(var * (1.0 / RWKV_HEAD) + GN_EPS) * ln_w + ln_b
        o_ref[sl, :] = ((yn + bonus * v) * g_ref[sl, :]).astype(o_ref.dtype)
        return carry

    lax.fori_loop(0, n_chunks, body, 0)

    @pl.when(tstep == pl.num_programs(2) - 1)
    def _():
        s_bd = s_scr[...]
        out = s_bd[0:RWKV_HEAD]
        for j in range(1, HEADS_PER_SLAB):
            out = out + s_bd[j * RWKV_HEAD:(j + 1) * RWKV_HEAD]
        sf_ref[...] = out


def _wkv(r, e, k, v, a, g, prm, s0):
    b, t_in, d = r.shape
    n = RWKV_HEAD
    nslab = d // SLAB
    c = WKV_CHUNK
    t = -(-t_in // c) * c
    if t != t_in:
        r, e, k, v, a, g = (jnp.pad(z, ((0, 0), (0, t - t_in), (0, 0))) for z in (r, e, k, v, a, g))
    tb = _pow2_tile(t, 512)
    s0_slab = s0.reshape(b, nslab, HEADS_PER_SLAB, n, n).transpose(0, 1, 3, 2, 4).reshape(b, nslab, n, SLAB)
    seq = pl.BlockSpec((None, tb, SLAB), lambda i, s, j: (i, j, s))
    st = pl.BlockSpec((None, None, n, SLAB), lambda i, s, j: (i, s, 0, 0))
    out, sf = pl.pallas_call(
        functools.partial(_wkv_kernel, chunk=c, n_chunks=tb // c),
        grid=(b, nslab, t // tb),
        in_specs=[seq] * 6 + [pl.BlockSpec((8, SLAB), lambda i, s, j: (0, s)), st],
        out_specs=[seq, st],
        out_shape=[jax.ShapeDtypeStruct((b, t, d), BF16), jax.ShapeDtypeStruct((b, nslab, n, SLAB), F32)],
        scratch_shapes=[pltpu.VMEM((SLAB, SLAB), F32)],
        compiler_params=_cparams(("parallel", "parallel", "arbitrary")),
        name="wkv7",
    )(r, e, k, v, a, g, prm, s0_slab)
    sf = sf.reshape(b, nslab, n, HEADS_PER_SLAB, n).transpose(0, 1, 3, 2, 4).reshape(b, d // n, n, n)
    return out[:, :t_in], sf


def _attn_prompt_kernel(q_ref, k_ref, v_ref, bias_ref, o_ref, kp, vp, *, t, scale):
    dh = kp.shape[1]
    zeros_past = jnp.zeros((BAND_PAST, dh), BF16)
    zeros_tail = jnp.zeros((KEY_WIN - BAND_PAST, dh), BF16)
    kp[0:BAND_PAST, :] = zeros_past
    vp[0:BAND_PAST, :] = zeros_past
    kp[BAND_PAST:BAND_PAST + t, :] = k_ref[...].astype(BF16)
    vp[BAND_PAST:BAND_PAST + t, :] = v_ref[...].astype(BF16)
    kp[BAND_PAST + t:, :] = zeros_tail
    vp[BAND_PAST + t:, :] = zeros_tail
    bias = bias_ref[...]
    col = lax.broadcasted_iota(jnp.int32, (CHUNK, KEY_WIN), 1)

    def body(c, carry):
        start = pl.multiple_of(c * CHUNK, CHUNK)
        q = q_ref[pl.ds(start, CHUNK), :]
        kw = kp[pl.ds(start, KEY_WIN), :]
        vw = vp[pl.ds(start, KEY_WIN), :]
        s = _dot_nt(q, kw) * scale + bias
        valid = (col >= BAND_PAST - c * CHUNK) & (col < BAND_PAST + CHUNK)
        s = jnp.where(valid, s, NEG_INF)
        m = jnp.max(s, axis=-1, keepdims=True)
        p = jnp.exp(s - m)
        p = p / jnp.sum(p, axis=-1, keepdims=True)
        o_ref[pl.ds(start, CHUNK), :] = _dot(p, vw).astype(o_ref.dtype)
        return carry

    lax.fori_loop(0, t // CHUNK, body, 0)


def _attn_prompt(qkv, bias, b, t):
    d = qkv.shape[1] // 3
    h = d // ATTN_HEAD_DIM
    col = lambda off: pl.BlockSpec((t, ATTN_HEAD_DIM), lambda i, j: (i, off + j))
    return pl.pallas_call(
        functools.partial(_attn_prompt_kernel, t=t, scale=ATTN_HEAD_DIM ** -0.5),
        grid=(b, h),
        in_specs=[col(0), col(h), col(2 * h),
                  pl.BlockSpec((None, CHUNK, KEY_WIN), lambda i, j: (j, 0, 0))],
        out_specs=col(0),
        out_shape=jax.ShapeDtypeStruct((b * t, d), BF16),
        scratch_shapes=[pltpu.VMEM((t + KEY_WIN, ATTN_HEAD_DIM), BF16)] * 2,
        compiler_params=_cparams(("parallel", "parallel")),
        name="band_attn_prompt",
    )(qkv, qkv, qkv, bias)


def _attn_sample_kernel(q_ref, kn_ref, vn_ref, kc_ref, vc_ref, bias_ref, o_ref, kf, vf, *, n_keys, scale):
    rows = kc_ref.shape[0]
    t = q_ref.shape[0]
    kf[0:rows, :] = kc_ref[...].astype(BF16)
    vf[0:rows, :] = vc_ref[...].astype(BF16)
    kf[rows:rows + t, :] = kn_ref[...].astype(BF16)
    vf[rows:rows + t, :] = vn_ref[...].astype(BF16)
    pad = jnp.zeros((kf.shape[0] - rows - t, kf.shape[1]), BF16)
    kf[rows + t:, :] = pad
    vf[rows + t:, :] = pad
    s = _dot_nt(q_ref[...], kf[...]) * scale + bias_ref[...]
    col = lax.broadcasted_iota(jnp.int32, s.shape, 1)
    s = jnp.where(col < n_keys, s, NEG_INF)
    m = jnp.max(s, axis=-1, keepdims=True)
    p = jnp.exp(s - m)
    p = p / jnp.sum(p, axis=-1, keepdims=True)
    o_ref[...] = _dot(p, vf[...]).astype(o_ref.dtype)


def _attn_sample(qkv, cache_k, cache_v, bias, b, t):
    d = qkv.shape[1] // 3
    h = d // ATTN_HEAD_DIM
    rows = cache_k.shape[1]
    win = bias.shape[2]
    col = lambda off: pl.BlockSpec((t, ATTN_HEAD_DIM), lambda i, j: (i, off + j))
    cache = pl.BlockSpec((None, rows, ATTN_HEAD_DIM), lambda i, j: (i, 0, j))
    return pl.pallas_call(
        functools.partial(_attn_sample_kernel, n_keys=rows + t, scale=ATTN_HEAD_DIM ** -0.5),
        grid=(b, h),
        in_specs=[col(0), col(h), col(2 * h), cache, cache,
                  pl.BlockSpec((None, t, win), lambda i, j: (j, 0, 0))],
        out_specs=col(0),
        out_shape=jax.ShapeDtypeStruct((b * t, d), BF16),
        scratch_shapes=[pltpu.VMEM((win, ATTN_HEAD_DIM), BF16)] * 2,
        compiler_params=_cparams(("parallel", "parallel")),
        name="band_attn_sample",
    )(qkv, qkv, qkv, cache_k, cache_v, bias)


def _rel_bias(table, diff):
    idx = jnp.clip(diff, -(CHUNK - 1), REL_MAX) + (CHUNK - 1)
    return table[:, idx].astype(F32)


def _trunk(x, p, shift0, wkv0, cache_k, cache_v, w, is_prompt):
    b, t, d = x.shape
    m = b * t
    depth = w["ln_g"].shape[0]
    alpha = (2 * depth) ** 0.25
    xf = x.reshape(m, d)
    xb = xf.astype(BF16)
    v_first = None
    shifts, wkvs, ks, vs = [], [], [], []

    def ffn(xf, xb, i, s, ln_idx):
        hidden = _ffn_up(xb, w["ffn_gate"][i][s], w["ffn_up"][i][s])
        y = _matmul(hidden, w["ffn_down"][i][s], F32, "ffn_down")
        return _add_ln(xf, y, w["ln_g"][i, ln_idx], w["ln_b"][i, ln_idx], alpha, 0.5)

    for i in range(depth):
        xf, xb = ffn(xf, xb, i, 0, 0)
        j = i // 2
        if i % 2 == 0:
            x3 = xf.reshape(b, t, d)
            xr, xw, xk, xv, xa, xg = _token_mixes(x3, shift0[:, j], w["rwkv_mu"][j])
            shifts.append(x3[:, -1])
            r = _matmul(xr, w["rwkv_w_r"][j], F32, "rwkv_r")
            k = _matmul(xk, w["rwkv_w_k"][j], F32, "rwkv_k")
            v = _matmul(xv, w["rwkv_w_v"][j], F32, "rwkv_v")
            e = _lora(xw, *w["rwkv_w12"][j], w["rwkv_w0"][j], "decay")
            a = _lora(xa, *w["rwkv_a12"][j], w["rwkv_a0"][j], "sigmoid")
            g = _lora(xg, *w["rwkv_g12"][j], jnp.zeros((d,), F32), "gate")
            if v_first is None:
                v_first = v
            else:
                v = _lora(xv, *w["rwkv_v12"][j - 1], w["rwkv_v0"][j - 1], "vmix", extra=(v, v_first))
            to3 = lambda z: z.reshape(b, t, d)
            mix, s_fin = _wkv(to3(r), to3(e), to3(k), to3(v), to3(a), to3(g), w["rwkv_prm"][j], wkv0[:, j])
            wkvs.append(s_fin)
            y = _matmul(mix.reshape(m, d), w["rwkv_w_o"][j], F32, "rwkv_o")
        else:
            qkv = _matmul(xb, w["attn_w_qkv"][j], F32, "attn_qkv")
            kv3 = qkv.reshape(b, t, 3 * d)
            h = d // ATTN_HEAD_DIM
            if is_prompt:
                rows = min(BAND_PAST, t)
                diff = jnp.arange(CHUNK)[:, None] + BAND_PAST - jnp.arange(KEY_WIN)[None, :]
                att = _attn_prompt(qkv, _rel_bias(w["attn_rel_bias"][j], diff), b, t)
                ks.append(kv3[:, t - rows:, d:2 * d].reshape(b, rows, h, ATTN_HEAD_DIM))
                vs.append(kv3[:, t - rows:, 2 * d:].reshape(b, rows, h, ATTN_HEAD_DIM))
            else:
                rows = cache_k.shape[2]
                win = -(-(rows + t) // LANES) * LANES
                diff = jnp.arange(t)[:, None] - (jnp.arange(win)[None, :] - rows)
                att = _attn_sample(qkv, cache_k[:, j].reshape(b, rows, d), cache_v[:, j].reshape(b, rows, d),
                                   _rel_bias(w["attn_rel_bias"][j], diff), b, t)
                ks.append(kv3[:, :, d:2 * d].reshape(b, t, h, ATTN_HEAD_DIM))
                vs.append(kv3[:, :, 2 * d:].reshape(b, t, h, ATTN_HEAD_DIM))
            y = _matmul(att, w["attn_w_o"][j], F32, "attn_o")
        xf, xb = _add_ln(xf, y, w["ln_g"][i, 1], w["ln_b"][i, 1], alpha, 1.0)
        xf, xb = ffn(xf, xb, i, 1, 2)
        xf, xb = _ple(xf, xb, p[i].reshape(m, -1), w["ple_w_gate"][i], w["ple_w_proj"][i])
    return (xf.reshape(b, t, d), jnp.stack(shifts, axis=1), jnp.stack(wkvs, axis=1),
            jnp.stack(ks, axis=1), jnp.stack(vs, axis=1))


def _prepare_weights(ln_g, ln_b, ffn_w_gate, ffn_w_up, ffn_w_down, ple_w_gate, ple_w_proj,
                     rwkv_mu, rwkv_w_r, rwkv_w_k, rwkv_w_v, rwkv_w_o, rwkv_w0, rwkv_w1, rwkv_w2,
                     rwkv_a0, rwkv_a1, rwkv_a2, rwkv_v0, rwkv_v1, rwkv_v2, rwkv_g1, rwkv_g2,
                     rwkv_k_k, rwkv_k_a, rwkv_r_k, rwkv_ln_w, rwkv_ln_b,
                     attn_w_qkv, attn_w_o, attn_rel_bias):
    depth = ln_g.shape[0]
    n_rwkv = rwkv_mu.shape[0]
    d = ln_g.shape[-1]
    f = ffn_w_gate.shape[-1]
    fp = -(-f // FF_ALIGN) * FF_ALIGN
    cast = lambda a: a.astype(BF16)
    pad_cols = lambda a: jnp.pad(cast(a), ((0, 0), (0, fp - f)))
    pad_rows = lambda a: jnp.pad(cast(a), ((0, fp - f), (0, 0)))
    zeros = jnp.zeros((d,), F32)
    return dict(
        ln_g=ln_g, ln_b=ln_b,
        ffn_gate=[[pad_cols(ffn_w_gate[i, s]) for s in range(2)] for i in range(depth)],
        ffn_up=[[pad_cols(ffn_w_up[i, s]) for s in range(2)] for i in range(depth)],
        ffn_down=[[pad_rows(ffn_w_down[i, s]) for s in range(2)] for i in range(depth)],
        ple_w_gate=cast(ple_w_gate), ple_w_proj=cast(ple_w_proj),
        rwkv_mu=rwkv_mu,
        rwkv_w_r=cast(rwkv_w_r), rwkv_w_k=cast(rwkv_w_k), rwkv_w_v=cast(rwkv_w_v), rwkv_w_o=cast(rwkv_w_o),
        rwkv_w0=rwkv_w0, rwkv_a0=rwkv_a0, rwkv_v0=rwkv_v0,
        rwkv_w12=[_pad_lora(rwkv_w1[j], rwkv_w2[j]) for j in range(n_rwkv)],
        rwkv_a12=[_pad_lora(rwkv_a1[j], rwkv_a2[j]) for j in range(n_rwkv)],
        rwkv_v12=[_pad_lora(rwkv_v1[j], rwkv_v2[j]) for j in range(rwkv_v1.shape[0])],
        rwkv_g12=[_pad_lora(rwkv_g1[j], rwkv_g2[j]) for j in range(n_rwkv)],
        rwkv_prm=[jnp.stack([rwkv_k_k[j], rwkv_k_a[j], rwkv_r_k[j].reshape(d), rwkv_ln_w[j], rwkv_ln_b[j],
                             zeros, zeros, zeros]) for j in range(n_rwkv)],
        attn_w_qkv=cast(attn_w_qkv), attn_w_o=cast(attn_w_o), attn_rel_bias=attn_rel_bias,
    )


def kernel(x_prompt, x_sample, state_shift, state_wkv, cache_k, cache_v, p_prompt, p_sample,
           ln_g, ln_b, ffn_w_gate, ffn_w_up, ffn_w_down, ple_w_gate, ple_w_proj,
           rwkv_mu, rwkv_w_r, rwkv_w_k, rwkv_w_v, rwkv_w_o, rwkv_w0, rwkv_w1, rwkv_w2,
           rwkv_a0, rwkv_a1, rwkv_a2, rwkv_v0, rwkv_v1, rwkv_v2, rwkv_g1, rwkv_g2,
           rwkv_k_k, rwkv_k_a, rwkv_r_k, rwkv_ln_w, rwkv_ln_b,
           attn_w_qkv, attn_w_o, attn_rel_bias):
    w = _prepare_weights(ln_g, ln_b, ffn_w_gate, ffn_w_up, ffn_w_down, ple_w_gate, ple_w_proj,
                         rwkv_mu, rwkv_w_r, rwkv_w_k, rwkv_w_v, rwkv_w_o, rwkv_w0, rwkv_w1, rwkv_w2,
                         rwkv_a0, rwkv_a1, rwkv_a2, rwkv_v0, rwkv_v1, rwkv_v2, rwkv_g1, rwkv_g2,
                         rwkv_k_k, rwkv_k_a, rwkv_r_k, rwkv_ln_w, rwkv_ln_b,
                         attn_w_qkv, attn_w_o, attn_rel_bias)
    bp, _, d = x_prompt.shape
    n_rwkv = rwkv_mu.shape[0]
    heads = d // RWKV_HEAD
    zero_shift = jnp.zeros((bp, n_rwkv, d), x_prompt.dtype)
    zero_wkv = jnp.zeros((bp, n_rwkv, heads, RWKV_HEAD, RWKV_HEAD), x_prompt.dtype)
    y_p, shift_p, wkv_p, k_p, v_p = _trunk(x_prompt, p_prompt, zero_shift, zero_wkv, None, None, w, True)
    y_s, shift_s, wkv_s, k_s, v_s = _trunk(x_sample, p_sample, state_shift, state_wkv, cache_k, cache_v, w, False)
    return (y_p, y_s, shift_p, wkv_p, k_p, v_p, shift_s, wkv_s, k_s, v_s)
```

```python
import functools
import math

import jax
import jax.numpy as jnp
from jax import lax
from jax.experimental import pallas as pl
from jax.experimental.pallas import tpu as pltpu

F32 = jnp.float32
BF16 = jnp.bfloat16

CHUNK = 64
BAND_CHUNKS = 8
BAND_PAST = BAND_CHUNKS * CHUNK
REL_MAX = 256
ATTN_HEAD_DIM = 128
RWKV_HEAD = 64
LN_EPS = 1e-5
GN_EPS = 64e-5
NEG_INF = -1e30

LANES = 128
SLAB = 256
HEADS_PER_SLAB = SLAB // RWKV_HEAD
WKV_CHUNK = 64
WKV_SLABS_PER_STEP = 4
Q_BLK = 2 * CHUNK
KEY_WIN = BAND_PAST + Q_BLK
ATTN_GROUP = 4
VMEM_LIMIT = 56 * 1024 * 1024


def _cparams(sem):
    return pltpu.CompilerParams(dimension_semantics=sem, vmem_limit_bytes=VMEM_LIMIT)


def _pow2_tile(m, cap):
    if m <= cap:
        return m
    t = 1 << (cap.bit_length() - 1)
    while m % t:
        t //= 2
    return t


def _lane_tile(n, cap):
    t = max(LANES, (cap // LANES) * LANES)
    while n % t:
        t -= LANES
    return t


def _dot(a, b):
    return jnp.dot(a.astype(BF16), b.astype(BF16), preferred_element_type=F32)


def _dot_nt(a, b):
    return lax.dot_general(a.astype(BF16), b.astype(BF16), (((1,), (1,)), ((), ())),
                           preferred_element_type=F32)


def _dot_tn(a, b):
    return lax.dot_general(a.astype(BF16), b.astype(BF16), (((0,), (0,)), ((), ())),
                           preferred_element_type=F32)


def _mm_kernel(x_ref, w_ref, o_ref):
    o_ref[...] = jnp.dot(x_ref[...], w_ref[...], preferred_element_type=F32).astype(o_ref.dtype)


def _mm_tiles(m, k, n):
    tm = _pow2_tile(m, max(8, (12 << 20) // (2 * k)))
    tn = _lane_tile(n, max(LANES, (6 << 20) // (2 * k)))
    return tm, tn


def _wspec(w, lead, k, tn):
    assert w.ndim == len(lead) + 2
    return pl.BlockSpec((None,) * len(lead) + (k, tn), lambda i, j: tuple(lead) + (0, j))


def _matmul(x, w, lead, out_dtype, name):
    m, k = x.shape
    n = w.shape[-1]
    tm, tn = _mm_tiles(m, k, n)
    return pl.pallas_call(
        _mm_kernel,
        grid=(m // tm, n // tn),
        in_specs=[pl.BlockSpec((tm, k), lambda i, j: (i, 0)), _wspec(w, lead, k, tn)],
        out_specs=pl.BlockSpec((tm, tn), lambda i, j: (i, j)),
        out_shape=jax.ShapeDtypeStruct((m, n), out_dtype),
        compiler_params=_cparams(("parallel", "arbitrary")),
        name=name,
    )(x, w)


def _ffn_up_kernel(x_ref, wg_ref, wu_ref, o_ref):
    x = x_ref[...]
    g = jnp.dot(x, wg_ref[...], preferred_element_type=F32)
    u = jnp.dot(x, wu_ref[...], preferred_element_type=F32)
    o_ref[...] = (g * jax.nn.sigmoid(g) * u).astype(o_ref.dtype)


def _ffn_up(xb, wg, wu, lead):
    m, k = xb.shape
    n = wg.shape[-1]
    tm = _pow2_tile(m, max(8, (16 << 20) // (2 * k)))
    tn = _lane_tile(n, max(LANES, (2 << 20) // (2 * k)))
    return pl.pallas_call(
        _ffn_up_kernel,
        grid=(m // tm, n // tn),
        in_specs=[pl.BlockSpec((tm, k), lambda i, j: (i, 0)), _wspec(wg, lead, k, tn), _wspec(wu, lead, k, tn)],
        out_specs=pl.BlockSpec((tm, tn), lambda i, j: (i, j)),
        out_shape=jax.ShapeDtypeStruct((m, n), BF16),
        compiler_params=_cparams(("parallel", "arbitrary")),
        name="ffn_up",
    )(xb, wg, wu)


def _add_ln_kernel(x_ref, y_ref, g_ref, b_ref, of_ref, ob_ref, *, alpha, scale):
    z = alpha * x_ref[...] + scale * y_ref[...]
    mu = jnp.mean(z, axis=-1, keepdims=True)
    zc = z - mu
    var = jnp.mean(zc * zc, axis=-1, keepdims=True)
    o = zc * lax.rsqrt(var + LN_EPS) * g_ref[...] + b_ref[...]
    of_ref[...] = o
    ob_ref[...] = o.astype(BF16)


def _add_ln(x, y, g, b, alpha, scale):
    m, d = x.shape
    tr = _pow2_tile(m, max(8, (4 << 20) // (4 * d)))
    row = pl.BlockSpec((tr, d), lambda i: (i, 0))
    vec = pl.BlockSpec((1, d), lambda i: (0, 0))
    return pl.pallas_call(
        functools.partial(_add_ln_kernel, alpha=alpha, scale=scale),
        grid=(m // tr,),
        in_specs=[row, row, vec, vec],
        out_specs=[row, row],
        out_shape=[jax.ShapeDtypeStruct((m, d), F32), jax.ShapeDtypeStruct((m, d), BF16)],
        compiler_params=_cparams(("parallel",)),
        name="add_ln",
    )(x, y, g.reshape(1, d), b.reshape(1, d))


def _ple_kernel(xb_ref, wg_ref, p_ref, wp_ref, x_ref, of_ref, ob_ref):
    gate = jax.nn.sigmoid(jnp.dot(xb_ref[...], wg_ref[...], preferred_element_type=F32))
    emb = jnp.dot(p_ref[...].astype(BF16), wp_ref[...], preferred_element_type=F32)
    o = x_ref[...] + gate * emb
    of_ref[...] = o
    ob_ref[...] = o.astype(BF16)


def _ple(x, xb, p, wg, wp, lead):
    m, d = x.shape
    pd = p.shape[1]
    tm = _pow2_tile(m, max(8, (8 << 20) // (2 * d)))
    tn = _lane_tile(d, max(LANES, (4 << 20) // (2 * d)))
    blk = pl.BlockSpec((tm, tn), lambda i, j: (i, j))
    return pl.pallas_call(
        _ple_kernel,
        grid=(m // tm, d // tn),
        in_specs=[pl.BlockSpec((tm, d), lambda i, j: (i, 0)), _wspec(wg, lead, d, tn),
                  pl.BlockSpec((tm, pd), lambda i, j: (i, 0)), _wspec(wp, lead, pd, tn), blk],
        out_specs=[blk, blk],
        out_shape=[jax.ShapeDtypeStruct((m, d), F32), jax.ShapeDtypeStruct((m, d), BF16)],
        compiler_params=_cparams(("parallel", "arbitrary")),
        name="ple",
    )(xb, wg, p, wp, x)


def _mix_kernel(x_ref, prev_ref, mu_ref, *o_refs):
    x = x_ref[...]
    shifted = pltpu.roll(x, 1, 0)
    first = lax.broadcasted_iota(jnp.int32, x.shape, 0) == 0
    shifted = jnp.where(first, prev_ref[...], shifted)
    xx = shifted - x
    for m, o_ref in enumerate(o_refs):
        o_ref[...] = (x + xx * mu_ref[m:m + 1, :]).astype(BF16)


def _token_mixes(x, x_prev, mu):
    b, t, d = x.shape
    tt = _pow2_tile(t, max(8, (4 << 20) // (4 * d)))
    nt = t // tt
    prev = jnp.concatenate([x_prev[:, None, :], x[:, tt - 1:t - 1:tt, :]], axis=1).reshape(b, nt, 1, d)
    blk = pl.BlockSpec((None, tt, d), lambda i, j: (i, j, 0))
    outs = pl.pallas_call(
        _mix_kernel,
        grid=(b, nt),
        in_specs=[blk,
                  pl.BlockSpec((None, None, 1, d), lambda i, j: (i, j, 0, 0)),
                  pl.BlockSpec((6, d), lambda i, j: (0, 0))],
        out_specs=[blk] * 6,
        out_shape=[jax.ShapeDtypeStruct((b, t, d), BF16)] * 6,
        compiler_params=_cparams(("parallel", "parallel")),
        name="token_mix",
    )(x, prev, mu)
    return [o.reshape(b * t, d) for o in outs]


def _lora_kernel(x_ref, w1_ref, w2_ref, *rest, mode):
    t = jnp.dot(x_ref[...], w1_ref[...], preferred_element_type=F32)
    if mode == "decay":
        t = jnp.tanh(t)
    elif mode == "gate":
        t = jax.nn.sigmoid(t)
    z = jnp.dot(t.astype(BF16), w2_ref[...], preferred_element_type=F32)
    if mode == "gate":
        (o_ref,) = rest
        o_ref[...] = z
    elif mode == "decay":
        bias_ref, o_ref = rest
        u = -(bias_ref[...] + z)
        softplus = jnp.maximum(u, 0.0) + jnp.log(1.0 + jnp.exp(-jnp.abs(u)))
        o_ref[...] = jnp.exp(-softplus - 0.5)
    elif mode == "sigmoid":
        bias_ref, o_ref = rest
        o_ref[...] = jax.nn.sigmoid(bias_ref[...] + z)
    else:
        bias_ref, v_ref, vf_ref, o_ref = rest
        v = v_ref[...]
        o_ref[...] = v + (vf_ref[...] - v) * jax.nn.sigmoid(bias_ref[...] + z)


def _lora(xb, w1, w2, bias, lead, mode, extra=()):
    m, d = xb.shape
    r = w1.shape[-1]
    tm = _pow2_tile(m, 256)
    row = pl.BlockSpec((tm, d), lambda i: (i, 0))
    whole = lambda a, b: pl.BlockSpec((None, a, b), lambda i: tuple(lead) + (0, 0))
    assert (bias is None) == (mode == "gate")
    biases = () if bias is None else (bias,)
    return pl.pallas_call(
        functools.partial(_lora_kernel, mode=mode),
        grid=(m // tm,),
        in_specs=[row, whole(d, r), whole(r, d)] + [whole(1, d)] * len(biases) + [row] * len(extra),
        out_specs=row,
        out_shape=jax.ShapeDtypeStruct((m, d), F32),
        compiler_params=_cparams(("parallel",)),
        name="lora_" + mode,
    )(xb, w1, w2, *biases, *extra)


def _pad_lora(w1, w2):
    r = w1.shape[-1]
    rp = -(-r // LANES) * LANES
    w1 = jnp.pad(w1.astype(BF16), ((0, 0), (0, 0), (0, rp - r)))
    w2 = jnp.pad(w2.astype(BF16), ((0, 0), (0, rp - r), (0, 0)))
    return w1, w2


def _round_robin(gens):
    while gens:
        alive = []
        for g in gens:
            try:
                next(g)
                alive.append(g)
            except StopIteration:
                pass
        gens = alive


def _split_bf16(x):
    hi = x.astype(BF16)
    lo = (x - hi.astype(F32)).astype(BF16)
    return hi, lo


def _wkv_kernel(r_ref, e_ref, k_ref, v_ref, a_ref, g_ref, prm_ref, s0_ref, o_ref, sf_ref, s_scr,
                *, chunk, n_chunks, n_slabs):
    c = chunk
    rows = HEADS_PER_SLAB * c
    tstep = pl.program_id(2)

    lane = lax.broadcasted_iota(jnp.int32, (1, SLAB), 1)
    head_masks = [(lane // RWKV_HEAD) == j for j in range(HEADS_PER_SLAB)]
    ri = lax.broadcasted_iota(jnp.int32, (rows, rows), 0)
    ci = lax.broadcasted_iota(jnp.int32, (rows, rows), 1)
    tri_strict = (ri % c) > (ci % c)
    tri_incl = (ri % c) >= (ci % c)
    eye = jnp.where(ri == ci, 1.0, 0.0).astype(F32)
    rs = lax.broadcasted_iota(jnp.int32, (SLAB, SLAB), 0)
    cs = lax.broadcasted_iota(jnp.int32, (SLAB, SLAB), 1)
    same_head = (rs // RWKV_HEAD) == (cs // RWKV_HEAD)
    ones_bd = jnp.where(same_head, 1.0, 0.0).astype(BF16)
    lt_incl = jnp.where(lax.broadcasted_iota(jnp.int32, (c, c), 0) >= lax.broadcasted_iota(jnp.int32, (c, c), 1),
                        1.0, 0.0).astype(BF16)

    @pl.when(tstep == 0)
    def _():
        for q in range(n_slabs):
            s0 = s0_ref[q]
            s_scr[q] = jnp.where(same_head, jnp.concatenate([s0] * HEADS_PER_SLAB, axis=0), 0.0)

    def stack(x):
        return jnp.concatenate([jnp.where(m, x, 0.0) for m in head_masks], axis=0)

    def fold(y):
        out = y[0:c]
        for j in range(1, HEADS_PER_SLAB):
            out = out + y[j * c:(j + 1) * c]
        return out

    def head_sums(xs):
        tot = jnp.dot(jnp.concatenate([x.astype(BF16) for x in xs], axis=0), ones_bd, preferred_element_type=F32)
        return [tot[i * c:(i + 1) * c] for i in range(len(xs))]

    n_sq = int(math.log2(c)) - 1

    def chunk_step(q, sl):
        lanes = slice(q * SLAB, (q + 1) * SLAB)
        prm = prm_ref[:, lanes]
        k_k, k_a, r_k, ln_w, ln_b = (prm[i:i + 1, :] for i in range(5))
        r = r_ref[sl, lanes]
        e = e_ref[sl, lanes]
        k = k_ref[sl, lanes]
        v = v_ref[sl, lanes]
        a = a_ref[sl, lanes]

        kk = k * k_k
        k2 = k * (1.0 + (a - 1.0) * k_a)
        ss, bonus = head_sums([kk * kk, r * k2 * r_k])
        kkn = kk / jnp.maximum(jnp.sqrt(ss), 1e-12)
        av = -kkn
        bv = kkn * a

        e_hi, e_lo = _split_bf16(e)
        cum = (jnp.dot(lt_incl, e_hi, preferred_element_type=F32)
               + jnp.dot(lt_incl, e_lo, preferred_element_type=F32))
        yield
        cum_last = cum[c - 1:c, :]
        p_in = jnp.exp(-cum)
        p_ex = jnp.exp(e - cum)
        p_inv = jnp.exp(cum)
        p_end = jnp.exp(cum - cum_last)
        p_c = jnp.exp(-cum_last)

        a_st = stack(av * p_ex)
        r_st = stack(r * p_in)
        b_st = stack(bv * p_inv)
        k_st = stack(k2 * p_inv)
        v_st = stack(v)
        ar = jnp.concatenate([a_st, r_st], axis=0).astype(BF16)
        xb = _dot_nt(ar, b_st)
        yield
        xk = _dot_nt(ar, k_st)
        yield
        a_ab = jnp.where(tri_strict, xb[:rows], 0.0)
        a_rb = jnp.where(tri_incl, xb[rows:], 0.0)
        a_ak = jnp.where(tri_strict, xk[:rows], 0.0)
        a_rk = jnp.where(tri_incl, xk[rows:], 0.0)

        x_pow = _dot(a_ab, a_ab)
        akv = _dot(a_ak, v_st)
        yield
        t_inv = eye + a_ab
        for _ in range(n_sq - 1):
            both = _dot(jnp.concatenate([t_inv, x_pow], axis=0), x_pow)
            yield
            t_inv = t_inv + both[:rows]
            x_pow = both[rows:]
        t_inv = t_inv + _dot(t_inv, x_pow)
        yield

        s_bd = s_scr[q]
        ars = _dot_nt(ar, s_bd)
        yield
        u_st = _dot(t_inv, ars[:rows] + akv)
        yield
        y = fold(ars[rows:] + _dot(a_rb, u_st) + _dot(a_rk, v_st))
        yield
        s_scr[q] = s_bd * p_c + _dot_tn(u_st, stack(bv * p_end)) + _dot_tn(v_st, stack(k2 * p_end))
        yield

        y_hi, y_lo = _split_bf16(y)
        mean_hi, mean_lo = head_sums([y_hi, y_lo])
        yield
        yc = y - (mean_hi + mean_lo) * (1.0 / RWKV_HEAD)
        (var,) = head_sums([yc * yc])
        yield
        yn = yc * lax.rsqrt(var * (1.0 / RWKV_HEAD) + GN_EPS) * ln_w + ln_b
        o_ref[sl, lanes] = ((yn + bonus * v) * g_ref[sl, lanes]).astype(o_ref.dtype)

    def body(idx, carry):
        sl = pl.ds(pl.multiple_of(idx * c, c), c)
        _round_robin([chunk_step(q, sl) for q in range(n_slabs)])
        return carry

    lax.fori_loop(0, n_chunks, body, 0)

    @pl.when(tstep == pl.num_programs(2) - 1)
    def _():
        for q in range(n_slabs):
            s_bd = s_scr[q]
            out = s_bd[0:RWKV_HEAD]
            for j in range(1, HEADS_PER_SLAB):
                out = out + s_bd[j * RWKV_HEAD:(j + 1) * RWKV_HEAD]
            sf_ref[q] = out


def _wkv(r, e, k, v, a, g, prm, s0):
    b, t_in, d = r.shape
    n = RWKV_HEAD
    nslab = d // SLAB
    c = WKV_CHUNK
    t = -(-t_in // c) * c
    if t != t_in:
        r, e, k, v, a, g = (jnp.pad(z, ((0, 0), (0, t - t_in), (0, 0))) for z in (r, e, k, v, a, g))
    tb = _pow2_tile(t, 512)
    ns = WKV_SLABS_PER_STEP if nslab % WKV_SLABS_PER_STEP == 0 else 1
    s0_slab = s0.reshape(b, nslab, HEADS_PER_SLAB, n, n).transpose(0, 1, 3, 2, 4).reshape(b, nslab, n, SLAB)
    seq = pl.BlockSpec((None, tb, ns * SLAB), lambda i, s, j: (i, j, s))
    st = pl.BlockSpec((None, ns, n, SLAB), lambda i, s, j: (i, s, 0, 0))
    out, sf = pl.pallas_call(
        functools.partial(_wkv_kernel, chunk=c, n_chunks=tb // c, n_slabs=ns),
        grid=(b, nslab // ns, t // tb),
        in_specs=[seq] * 6 + [pl.BlockSpec((8, ns * SLAB), lambda i, s, j: (0, s)), st],
        out_specs=[seq, st],
        out_shape=[jax.ShapeDtypeStruct((b, t, d), BF16), jax.ShapeDtypeStruct((b, nslab, n, SLAB), F32)],
        scratch_shapes=[pltpu.VMEM((ns, SLAB, SLAB), F32)],
        compiler_params=_cparams(("parallel", "parallel", "arbitrary")),
        name="wkv7",
    )(r, e, k, v, a, g, prm, s0_slab)
    sf = sf.reshape(b, nslab, n, HEADS_PER_SLAB, n).transpose(0, 1, 3, 2, 4).reshape(b, d // n, n, n)
    return out[:, :t_in], sf


def _attn_prompt_kernel(q_ref, k_ref, v_ref, bias_ref, o_ref, kp, vp, *, t, scale):
    dh = kp.shape[1]
    zeros_past = jnp.zeros((BAND_PAST, dh), BF16)
    kp[0:BAND_PAST, :] = zeros_past
    vp[0:BAND_PAST, :] = zeros_past
    kp[BAND_PAST:, :] = k_ref[...].astype(BF16)
    vp[BAND_PAST:, :] = v_ref[...].astype(BF16)
    col = lax.broadcasted_iota(jnp.int32, (Q_BLK, KEY_WIN), 1)

    def block(c, mask_start):
        start = pl.multiple_of(c * Q_BLK, Q_BLK)
        s = _dot_nt(q_ref[pl.ds(start, Q_BLK), :], kp[pl.ds(start, KEY_WIN), :])
        yield
        s = s * scale + bias_ref[...]
        if mask_start:
            s = jnp.where(col >= BAND_PAST - c * Q_BLK, s, NEG_INF)
        m = jnp.max(s, axis=-1, keepdims=True)
        p = jnp.exp(s - m)
        p = p / jnp.sum(p, axis=-1, keepdims=True)
        o = _dot(p, vp[pl.ds(start, KEY_WIN), :])
        yield
        o_ref[pl.ds(start, Q_BLK), :] = o.astype(o_ref.dtype)

    def make_body(mask_start):
        def body(g, carry):
            _round_robin([block(g * ATTN_GROUP + i, mask_start) for i in range(ATTN_GROUP)])
            return carry
        return body

    n_grp = t // (Q_BLK * ATTN_GROUP)
    n_front = min(-(-BAND_PAST // (Q_BLK * ATTN_GROUP)), n_grp)
    lax.fori_loop(0, n_front, make_body(True), 0)
    lax.fori_loop(n_front, n_grp, make_body(False), 0)


def _attn_prompt(qkv, bias, b, t):
    d = qkv.shape[1] // 3
    h = d // ATTN_HEAD_DIM
    assert t % (ATTN_GROUP * Q_BLK) == 0
    col = lambda off: pl.BlockSpec((t, ATTN_HEAD_DIM), lambda i, j: (i, off + j))
    return pl.pallas_call(
        functools.partial(_attn_prompt_kernel, t=t, scale=ATTN_HEAD_DIM ** -0.5),
        grid=(b, h),
        in_specs=[col(0), col(h), col(2 * h),
                  pl.BlockSpec((None, Q_BLK, KEY_WIN), lambda i, j: (j, 0, 0))],
        out_specs=col(0),
        out_shape=jax.ShapeDtypeStruct((b * t, d), BF16),
        scratch_shapes=[pltpu.VMEM((BAND_PAST + t, ATTN_HEAD_DIM), BF16)] * 2,
        compiler_params=_cparams(("parallel", "parallel")),
        name="band_attn_prompt",
    )(qkv, qkv, qkv, bias)


def _attn_sample_kernel(q_ref, kn_ref, vn_ref, kc_ref, vc_ref, bias_ref, o_ref, kf, vf, *, n_keys, scale):
    rows = kc_ref.shape[0]
    t = q_ref.shape[0]
    kf[0:rows, :] = kc_ref[...].astype(BF16)
    vf[0:rows, :] = vc_ref[...].astype(BF16)
    kf[rows:rows + t, :] = kn_ref[...].astype(BF16)
    vf[rows:rows + t, :] = vn_ref[...].astype(BF16)
    pad = jnp.zeros((kf.shape[0] - rows - t, kf.shape[1]), BF16)
    kf[rows + t:, :] = pad
    vf[rows + t:, :] = pad
    s = _dot_nt(q_ref[...], kf[...]) * scale + bias_ref[...]
    col = lax.broadcasted_iota(jnp.int32, s.shape, 1)
    s = jnp.where(col < n_keys, s, NEG_INF)
    m = jnp.max(s, axis=-1, keepdims=True)
    p = jnp.exp(s - m)
    p = p / jnp.sum(p, axis=-1, keepdims=True)
    o_ref[...] = _dot(p, vf[...]).astype(o_ref.dtype)


def _attn_sample(qkv, cache_k, cache_v, bias, b, t):
    d = qkv.shape[1] // 3
    h = d // ATTN_HEAD_DIM
    rows = cache_k.shape[1]
    win = bias.shape[2]
    col = lambda off: pl.BlockSpec((t, ATTN_HEAD_DIM), lambda i, j: (i, off + j))
    cache = pl.BlockSpec((None, rows, ATTN_HEAD_DIM), lambda i, j: (i, 0, j))
    return pl.pallas_call(
        functools.partial(_attn_sample_kernel, n_keys=rows + t, scale=ATTN_HEAD_DIM ** -0.5),
        grid=(b, h),
        in_specs=[col(0), col(h), col(2 * h), cache, cache,
                  pl.BlockSpec((None, t, win), lambda i, j: (j, 0, 0))],
        out_specs=col(0),
        out_shape=jax.ShapeDtypeStruct((b * t, d), BF16),
        scratch_shapes=[pltpu.VMEM((win, ATTN_HEAD_DIM), BF16)] * 2,
        compiler_params=_cparams(("parallel", "parallel")),
        name="band_attn_sample",
    )(qkv, qkv, qkv, cache_k, cache_v, bias)


def _rel_bias(table, diff):
    idx = jnp.clip(diff, -(CHUNK - 1), REL_MAX) + (CHUNK - 1)
    return table[:, idx].astype(F32)


def _trunk(x, p, shift0, wkv0, cache_k, cache_v, w, is_prompt):
    b, t, d = x.shape
    m = b * t
    depth = w["ln_g"].shape[0]
    alpha = (2 * depth) ** 0.25
    xf = x.reshape(m, d)
    xb = xf.astype(BF16)
    v_first = None
    shifts, wkvs, ks, vs = [], [], [], []

    def ffn(xf, xb, i, s, ln_idx):
        hidden = _ffn_up(xb, w["ffn_gate"], w["ffn_up"], (i, s))
        y = _matmul(hidden, w["ffn_down"], (i, s), F32, "ffn_down")
        return _add_ln(xf, y, w["ln_g"][i, ln_idx], w["ln_b"][i, ln_idx], alpha, 0.5)

    for i in range(depth):
        xf, xb = ffn(xf, xb, i, 0, 0)
        j = i // 2
        if i % 2 == 0:
            x3 = xf.reshape(b, t, d)
            xr, xw, xk, xv, xa, xg = _token_mixes(x3, shift0[:, j], w["rwkv_mu"][j])
            shifts.append(x3[:, -1])
            r = _matmul(xr, w["rwkv_w_r"], (j,), F32, "rwkv_r")
            k = _matmul(xk, w["rwkv_w_k"], (j,), F32, "rwkv_k")
            v = _matmul(xv, w["rwkv_w_v"], (j,), F32, "rwkv_v")
            e = _lora(xw, *w["rwkv_w12"], w["rwkv_w0"], (j,), "decay")
            a = _lora(xa, *w["rwkv_a12"], w["rwkv_a0"], (j,), "sigmoid")
            g = _lora(xg, *w["rwkv_g12"], None, (j,), "gate")
            if v_first is None:
                v_first = v
            else:
                v = _lora(xv, *w["rwkv_v12"], w["rwkv_v0"], (j - 1,), "vmix", extra=(v, v_first))
            to3 = lambda z: z.reshape(b, t, d)
            mix, s_fin = _wkv(to3(r), to3(e), to3(k), to3(v), to3(a), to3(g), w["rwkv_prm"][j], wkv0[:, j])
            wkvs.append(s_fin)
            y = _matmul(mix.reshape(m, d), w["rwkv_w_o"], (j,), F32, "rwkv_o")
        else:
            qkv = _matmul(xb, w["attn_w_qkv"], (j,), F32, "attn_qkv")
            kv3 = qkv.reshape(b, t, 3 * d)
            h = d // ATTN_HEAD_DIM
            if is_prompt:
                rows = min(BAND_PAST, t)
                qi = jnp.arange(Q_BLK)[:, None]
                kj = jnp.arange(KEY_WIN)[None, :]
                in_band = jnp.where(qi < CHUNK, kj < BAND_PAST + CHUNK, kj >= CHUNK)
                bias = jnp.where(in_band[None], _rel_bias(w["attn_rel_bias"][j], qi + BAND_PAST - kj), NEG_INF)
                att = _attn_prompt(qkv, bias, b, t)
                ks.append(kv3[:, t - rows:, d:2 * d].reshape(b, rows, h, ATTN_HEAD_DIM))
                vs.append(kv3[:, t - rows:, 2 * d:].reshape(b, rows, h, ATTN_HEAD_DIM))
            else:
                rows = cache_k.shape[2]
                win = -(-(rows + t) // LANES) * LANES
                diff = jnp.arange(t)[:, None] - (jnp.arange(win)[None, :] - rows)
                att = _attn_sample(qkv, cache_k[:, j].reshape(b, rows, d), cache_v[:, j].reshape(b, rows, d),
                                   _rel_bias(w["attn_rel_bias"][j], diff), b, t)
                ks.append(kv3[:, :, d:2 * d].reshape(b, t, h, ATTN_HEAD_DIM))
                vs.append(kv3[:, :, 2 * d:].reshape(b, t, h, ATTN_HEAD_DIM))
            y = _matmul(att, w["attn_w_o"], (j,), F32, "attn_o")
        xf, xb = _add_ln(xf, y, w["ln_g"][i, 1], w["ln_b"][i, 1], alpha, 1.0)
        xf, xb = ffn(xf, xb, i, 1, 2)
        xf, xb = _ple(xf, xb, p[i].reshape(m, -1), w["ple_w_gate"], w["ple_w_proj"], (i,))
    return (xf.reshape(b, t, d), jnp.stack(shifts, axis=1), jnp.stack(wkvs, axis=1),
            jnp.stack(ks, axis=1), jnp.stack(vs, axis=1))


def _prepare_weights(ln_g, ln_b, ffn_w_gate, ffn_w_up, ffn_w_down, ple_w_gate, ple_w_proj,
                     rwkv_mu, rwkv_w_r, rwkv_w_k, rwkv_w_v, rwkv_w_o, rwkv_w0, rwkv_w1, rwkv_w2,
                     rwkv_a0, rwkv_a1, rwkv_a2, rwkv_v0, rwkv_v1, rwkv_v2, rwkv_g1, rwkv_g2,
                     rwkv_k_k, rwkv_k_a, rwkv_r_k, rwkv_ln_w, rwkv_ln_b,
                     attn_w_qkv, attn_w_o, attn_rel_bias):
    n_rwkv = rwkv_mu.shape[0]
    d = ln_g.shape[-1]
    cast = lambda a: a.astype(BF16)
    row = lambda a: a.reshape(a.shape[0], 1, d)
    zeros = jnp.zeros((n_rwkv, d), F32)
    return dict(
        ln_g=ln_g, ln_b=ln_b,
        ffn_gate=cast(ffn_w_gate), ffn_up=cast(ffn_w_up), ffn_down=cast(ffn_w_down),
        ple_w_gate=cast(ple_w_gate), ple_w_proj=cast(ple_w_proj),
        rwkv_mu=rwkv_mu,
        rwkv_w_r=cast(rwkv_w_r), rwkv_w_k=cast(rwkv_w_k), rwkv_w_v=cast(rwkv_w_v), rwkv_w_o=cast(rwkv_w_o),
        rwkv_w0=row(rwkv_w0), rwkv_a0=row(rwkv_a0), rwkv_v0=row(rwkv_v0),
        rwkv_w12=_pad_lora(rwkv_w1, rwkv_w2), rwkv_a12=_pad_lora(rwkv_a1, rwkv_a2),
        rwkv_v12=_pad_lora(rwkv_v1, rwkv_v2), rwkv_g12=_pad_lora(rwkv_g1, rwkv_g2),
        rwkv_prm=jnp.stack([rwkv_k_k, rwkv_k_a, rwkv_r_k.reshape(n_rwkv, d), rwkv_ln_w, rwkv_ln_b,
                            zeros, zeros, zeros], axis=1),
        attn_w_qkv=cast(attn_w_qkv), attn_w_o=cast(attn_w_o), attn_rel_bias=attn_rel_bias,
    )


def kernel(x_prompt, x_sample, state_shift, state_wkv, cache_k, cache_v, p_prompt, p_sample,
           ln_g, ln_b, ffn_w_gate, ffn_w_up, ffn_w_down, ple_w_gate, ple_w_proj,
           rwkv_mu, rwkv_w_r, rwkv_w_k, rwkv_w_v, rwkv_w_o, rwkv_w0, rwkv_w1, rwkv_w2,
           rwkv_a0, rwkv_a1, rwkv_a2, rwkv_v0, rwkv_v1, rwkv_v2, rwkv_g1, rwkv_g2,
           rwkv_k_k, rwkv_k_a, rwkv_r_k, rwkv_ln_w, rwkv_ln_b,
           attn_w_qkv, attn_w_o, attn_rel_bias):
    w = _prepare_weights(ln_g, ln_b, ffn_w_gate, ffn_w_up, ffn_w_down, ple_w_gate, ple_w_proj,
                         rwkv_mu, rwkv_w_r, rwkv_w_k, rwkv_w_v, rwkv_w_o, rwkv_w0, rwkv_w1, rwkv_w2,
                         rwkv_a0, rwkv_a1, rwkv_a2, rwkv_v0, rwkv_v1, rwkv_v2, rwkv_g1, rwkv_g2,
                         rwkv_k_k, rwkv_k_a, rwkv_r_k, rwkv_ln_w, rwkv_ln_b,
                         attn_w_qkv, attn_w_o, attn_rel_bias)
    bp, _, d = x_prompt.shape
    n_rwkv = rwkv_mu.shape[0]
    heads = d // RWKV_HEAD
    zero_shift = jnp.zeros((bp, n_rwkv, d), x_prompt.dtype)
    zero_wkv = jnp.zeros((bp, n_rwkv, heads, RWKV_HEAD, RWKV_HEAD), x_prompt.dtype)
    y_p, shift_p, wkv_p, k_p, v_p = _trunk(x_prompt, p_prompt, zero_shift, zero_wkv, None, None, w, True)
    y_s, shift_s, wkv_s, k_s, v_s = _trunk(x_sample, p_sample, state_shift, state_wkv, cache_k, cache_v, w, False)
    return (y_p, y_s, shift_p, wkv_p, k_p, v_p, shift_s, wkv_s, k_s, v_s)
```

```python
import functools
import math

import jax
import jax.numpy as jnp
from jax import lax
from jax.experimental import pallas as pl
from jax.experimental.pallas import tpu as pltpu

F32 = jnp.float32
BF16 = jnp.bfloat16

CHUNK = 64
BAND_CHUNKS = 8
BAND_PAST = BAND_CHUNKS * CHUNK
REL_MAX = 256
ATTN_HEAD_DIM = 128
RWKV_HEAD = 64
LN_EPS = 1e-5
GN_EPS = 64e-5
NEG_INF = -1e30

LANES = 128
SLAB = 256
HEADS_PER_SLAB = SLAB // RWKV_HEAD
WKV_CHUNK = 64
WKV_SLABS_PER_STEP = 4
Q_BLK = 2 * CHUNK
KEY_WIN = BAND_PAST + Q_BLK
ATTN_GROUP = 4
VMEM_LIMIT = 56 * 1024 * 1024
MM_VMEM_BUDGET = 50 * 1024 * 1024


def _cparams(sem):
    return pltpu.CompilerParams(dimension_semantics=sem, vmem_limit_bytes=VMEM_LIMIT)


def _pow2_tile(m, cap):
    if m <= cap:
        return m
    t = 1 << (cap.bit_length() - 1)
    while m % t:
        t //= 2
    return t


def _lane_tile(n, cap):
    t = max(LANES, (cap // LANES) * LANES)
    while n % t:
        t -= LANES
    return t


def _dot(a, b):
    return jnp.dot(a.astype(BF16), b.astype(BF16), preferred_element_type=F32)


def _dot_nt(a, b):
    return lax.dot_general(a.astype(BF16), b.astype(BF16), (((1,), (1,)), ((), ())),
                           preferred_element_type=F32)


def _dot_tn(a, b):
    return lax.dot_general(a.astype(BF16), b.astype(BF16), (((0,), (0,)), ((), ())),
                           preferred_element_type=F32)


def _mm_kernel(x_ref, w_ref, o_ref):
    o_ref[...] = jnp.dot(x_ref[...], w_ref[...], preferred_element_type=F32).astype(o_ref.dtype)


def _mm_tiles(m, k, n, n_f32_blocks):
    tm = _pow2_tile(m, max(8, (12 << 20) // (2 * k)))
    tn = n
    while tn > LANES and (n % tn or 2 * (2 * tm * k + 2 * k * tn + 4 * n_f32_blocks * tm * tn) > MM_VMEM_BUDGET):
        tn -= LANES
    return tm, tn


def _wspec(w, lead, k, tn):
    assert w.ndim == len(lead) + 2
    return pl.BlockSpec((None,) * len(lead) + (k, tn), lambda i, j: tuple(lead) + (0, j))


def _mm_residual_kernel(x_ref, w_ref, r_ref, o_ref, *, alpha, scale):
    o_ref[...] = alpha * r_ref[...] + scale * jnp.dot(x_ref[...], w_ref[...], preferred_element_type=F32)


def _matmul(x, w, lead, out_dtype, name, residual=None):
    m, k = x.shape
    n = w.shape[-1]
    tm, tn = _mm_tiles(m, k, n, 1 if residual is None else 2)
    out_blk = pl.BlockSpec((tm, tn), lambda i, j: (i, j))
    in_specs = [pl.BlockSpec((tm, k), lambda i, j: (i, 0)), _wspec(w, lead, k, tn)]
    if residual is None:
        body, args = _mm_kernel, (x, w)
    else:
        r, alpha, scale = residual
        body, args = functools.partial(_mm_residual_kernel, alpha=alpha, scale=scale), (x, w, r)
        in_specs.append(out_blk)
    return pl.pallas_call(
        body,
        grid=(m // tm, n // tn),
        in_specs=in_specs,
        out_specs=out_blk,
        out_shape=jax.ShapeDtypeStruct((m, n), out_dtype),
        compiler_params=_cparams(("parallel", "arbitrary")),
        name=name,
    )(*args)


def _ffn_up_kernel(x_ref, wg_ref, wu_ref, o_ref):
    x = x_ref[...]
    g = jnp.dot(x, wg_ref[...], preferred_element_type=F32)
    u = jnp.dot(x, wu_ref[...], preferred_element_type=F32)
    o_ref[...] = (g * jax.nn.sigmoid(g) * u).astype(o_ref.dtype)


def _ffn_up(xb, wg, wu, lead):
    m, k = xb.shape
    n = wg.shape[-1]
    tm = _pow2_tile(m, max(8, (16 << 20) // (2 * k)))
    tn = _lane_tile(n, max(LANES, (2 << 20) // (2 * k)))
    return pl.pallas_call(
        _ffn_up_kernel,
        grid=(m // tm, n // tn),
        in_specs=[pl.BlockSpec((tm, k), lambda i, j: (i, 0)), _wspec(wg, lead, k, tn), _wspec(wu, lead, k, tn)],
        out_specs=pl.BlockSpec((tm, tn), lambda i, j: (i, j)),
        out_shape=jax.ShapeDtypeStruct((m, n), BF16),
        compiler_params=_cparams(("parallel", "arbitrary")),
        name="ffn_up",
    )(xb, wg, wu)


def _ln_kernel(z_ref, g_ref, b_ref, of_ref, ob_ref):
    z = z_ref[...]
    mu = jnp.mean(z, axis=-1, keepdims=True)
    zc = z - mu
    var = jnp.mean(zc * zc, axis=-1, keepdims=True)
    o = zc * lax.rsqrt(var + LN_EPS) * g_ref[...] + b_ref[...]
    of_ref[...] = o
    ob_ref[...] = o.astype(BF16)


def _layer_norm(z, g, b):
    m, d = z.shape
    tr = _pow2_tile(m, max(8, (4 << 20) // (4 * d)))
    row = pl.BlockSpec((tr, d), lambda i: (i, 0))
    vec = pl.BlockSpec((1, d), lambda i: (0, 0))
    return pl.pallas_call(
        _ln_kernel,
        grid=(m // tr,),
        in_specs=[row, vec, vec],
        out_specs=[row, row],
        out_shape=[jax.ShapeDtypeStruct((m, d), F32), jax.ShapeDtypeStruct((m, d), BF16)],
        compiler_params=_cparams(("parallel",)),
        name="layer_norm",
    )(z, g.reshape(1, d), b.reshape(1, d))


def _ple_kernel(xb_ref, wg_ref, p_ref, wp_ref, x_ref, of_ref, ob_ref):
    gate = jax.nn.sigmoid(jnp.dot(xb_ref[...], wg_ref[...], preferred_element_type=F32))
    emb = jnp.dot(p_ref[...].astype(BF16), wp_ref[...], preferred_element_type=F32)
    o = x_ref[...] + gate * emb
    of_ref[...] = o
    ob_ref[...] = o.astype(BF16)


def _ple(x, xb, p, wg, wp, lead):
    m, d = x.shape
    pd = p.shape[-1]
    tm = _pow2_tile(m, max(8, (8 << 20) // (2 * d)))
    tn = _lane_tile(d, max(LANES, (4 << 20) // (2 * d)))
    blk = pl.BlockSpec((tm, tn), lambda i, j: (i, j))
    return pl.pallas_call(
        _ple_kernel,
        grid=(m // tm, d // tn),
        in_specs=[pl.BlockSpec((tm, d), lambda i, j: (i, 0)), _wspec(wg, lead, d, tn),
                  pl.BlockSpec((None, tm, pd), lambda i, j: tuple(lead) + (i, 0)), _wspec(wp, lead, pd, tn), blk],
        out_specs=[blk, blk],
        out_shape=[jax.ShapeDtypeStruct((m, d), F32), jax.ShapeDtypeStruct((m, d), BF16)],
        compiler_params=_cparams(("parallel", "arbitrary")),
        name="ple",
    )(xb, wg, p, wp, x)


def _mix_kernel(x_ref, prev_ref, mu_ref, *o_refs):
    x = x_ref[...]
    shifted = pltpu.roll(x, 1, 0)
    first = lax.broadcasted_iota(jnp.int32, x.shape, 0) == 0
    shifted = jnp.where(first, prev_ref[...], shifted)
    xx = shifted - x
    for m, o_ref in enumerate(o_refs):
        o_ref[...] = (x + xx * mu_ref[m:m + 1, :]).astype(BF16)


def _token_mixes(x, x_prev, mu):
    b, t, d = x.shape
    tt = _pow2_tile(t, max(8, (4 << 20) // (4 * d)))
    nt = t // tt
    prev = jnp.concatenate([x_prev[:, None, :], x[:, tt - 1:t - 1:tt, :]], axis=1).reshape(b, nt, 1, d)
    blk = pl.BlockSpec((None, tt, d), lambda i, j: (i, j, 0))
    outs = pl.pallas_call(
        _mix_kernel,
        grid=(b, nt),
        in_specs=[blk,
                  pl.BlockSpec((None, None, 1, d), lambda i, j: (i, j, 0, 0)),
                  pl.BlockSpec((6, d), lambda i, j: (0, 0))],
        out_specs=[blk] * 6,
        out_shape=[jax.ShapeDtypeStruct((b, t, d), BF16)] * 6,
        compiler_params=_cparams(("parallel", "parallel")),
        name="token_mix",
    )(x, prev, mu)
    return [o.reshape(b * t, d) for o in outs]


def _lora_kernel(x_ref, w1_ref, w2_ref, *rest, mode):
    t = jnp.dot(x_ref[...], w1_ref[...], preferred_element_type=F32)
    if mode == "decay":
        t = jnp.tanh(t)
    elif mode == "gate":
        t = jax.nn.sigmoid(t)
    z = jnp.dot(t.astype(BF16), w2_ref[...], preferred_element_type=F32)
    if mode == "gate":
        (o_ref,) = rest
        o_ref[...] = z
    elif mode == "decay":
        bias_ref, o_ref = rest
        u = -(bias_ref[...] + z)
        softplus = jnp.maximum(u, 0.0) + jnp.log(1.0 + jnp.exp(-jnp.abs(u)))
        o_ref[...] = jnp.exp(-softplus - 0.5)
    elif mode == "sigmoid":
        bias_ref, o_ref = rest
        o_ref[...] = jax.nn.sigmoid(bias_ref[...] + z)
    else:
        bias_ref, v_ref, vf_ref, o_ref = rest
        v = v_ref[...]
        o_ref[...] = v + (vf_ref[...] - v) * jax.nn.sigmoid(bias_ref[...] + z)


def _lora(xb, w1, w2, bias, lead, mode, extra=()):
    m, d = xb.shape
    r = w1.shape[-1]
    tm = _pow2_tile(m, 256)
    row = pl.BlockSpec((tm, d), lambda i: (i, 0))
    whole = lambda a, b: pl.BlockSpec((None, a, b), lambda i: tuple(lead) + (0, 0))
    assert (bias is None) == (mode == "gate")
    biases = () if bias is None else (bias,)
    return pl.pallas_call(
        functools.partial(_lora_kernel, mode=mode),
        grid=(m // tm,),
        in_specs=[row, whole(d, r), whole(r, d)] + [whole(1, d)] * len(biases) + [row] * len(extra),
        out_specs=row,
        out_shape=jax.ShapeDtypeStruct((m, d), F32),
        compiler_params=_cparams(("parallel",)),
        name="lora_" + mode,
    )(xb, w1, w2, *biases, *extra)


def _pad_lora(w1, w2):
    r = w1.shape[-1]
    rp = -(-r // LANES) * LANES
    w1 = jnp.pad(w1.astype(BF16), ((0, 0), (0, 0), (0, rp - r)))
    w2 = jnp.pad(w2.astype(BF16), ((0, 0), (0, rp - r), (0, 0)))
    return w1, w2


def _round_robin(gens):
    while gens:
        alive = []
        for g in gens:
            try:
                next(g)
                alive.append(g)
            except StopIteration:
                pass
        gens = alive


def _split_bf16(x):
    hi = x.astype(BF16)
    lo = (x - hi.astype(F32)).astype(BF16)
    return hi, lo


def _wkv_kernel(r_ref, e_ref, k_ref, v_ref, a_ref, g_ref, prm_ref, s0_ref, o_ref, sf_ref, s_scr,
                *, chunk, n_chunks, n_slabs):
    c = chunk
    rows = HEADS_PER_SLAB * c
    tstep = pl.program_id(2)

    lane = lax.broadcasted_iota(jnp.int32, (1, SLAB), 1)
    head_masks = [(lane // RWKV_HEAD) == j for j in range(HEADS_PER_SLAB)]
    ri = lax.broadcasted_iota(jnp.int32, (rows, rows), 0)
    ci = lax.broadcasted_iota(jnp.int32, (rows, rows), 1)
    tri_strict = (ri % c) > (ci % c)
    tri_incl_cat = (lax.broadcasted_iota(jnp.int32, (c, rows), 0)
                    >= lax.broadcasted_iota(jnp.int32, (c, rows), 1) % c)
    eye = jnp.where(ri == ci, 1.0, 0.0).astype(F32)
    rs = lax.broadcasted_iota(jnp.int32, (SLAB, SLAB), 0)
    cs = lax.broadcasted_iota(jnp.int32, (SLAB, SLAB), 1)
    same_head = (rs // RWKV_HEAD) == (cs // RWKV_HEAD)
    ones_bd = jnp.where(same_head, 1.0, 0.0).astype(BF16)
    lt_incl = jnp.where(lax.broadcasted_iota(jnp.int32, (c, c), 0) >= lax.broadcasted_iota(jnp.int32, (c, c), 1),
                        1.0, 0.0).astype(BF16)

    @pl.when(tstep == 0)
    def _():
        for q in range(n_slabs):
            s0 = s0_ref[q]
            s_scr[q] = jnp.where(same_head, jnp.concatenate([s0] * HEADS_PER_SLAB, axis=0), 0.0)

    def stack(x):
        return jnp.concatenate([jnp.where(m, x, 0.0) for m in head_masks], axis=0)

    def fold(y):
        out = y[0:c]
        for j in range(1, HEADS_PER_SLAB):
            out = out + y[j * c:(j + 1) * c]
        return out

    def head_sums(xs):
        tot = jnp.dot(jnp.concatenate([x.astype(BF16) for x in xs], axis=0), ones_bd, preferred_element_type=F32)
        return [tot[i * c:(i + 1) * c] for i in range(len(xs))]

    n_sq = int(math.log2(c)) - 1

    def chunk_step(q, sl):
        lanes = slice(q * SLAB, (q + 1) * SLAB)
        prm = prm_ref[:, lanes]
        k_k, k_a, r_k, ln_w, ln_b = (prm[i:i + 1, :] for i in range(5))
        r = r_ref[sl, lanes]
        e = e_ref[sl, lanes]
        k = k_ref[sl, lanes]
        v = v_ref[sl, lanes]
        a = a_ref[sl, lanes]

        kk = k * k_k
        k2 = k * (1.0 + (a - 1.0) * k_a)
        ss, bonus = head_sums([kk * kk, r * k2 * r_k])
        kkn = kk / jnp.maximum(jnp.sqrt(ss), 1e-12)
        av = -kkn
        bv = kkn * a

        e_hi, e_lo = _split_bf16(e)
        cum = (jnp.dot(lt_incl, e_hi, preferred_element_type=F32)
               + jnp.dot(lt_incl, e_lo, preferred_element_type=F32))
        yield
        cum_last = cum[c - 1:c, :]
        p_in = jnp.exp(-cum)
        p_ex = jnp.exp(e - cum)
        p_inv = jnp.exp(cum)
        p_end = jnp.exp(cum - cum_last)
        p_c = jnp.exp(-cum_last)

        b_st = stack(bv * p_inv)
        k_st = stack(k2 * p_inv)
        v_st = stack(v)
        ar = jnp.concatenate([stack(av * p_ex), r * p_in], axis=0).astype(BF16)
        xb = _dot_nt(ar, b_st)
        yield
        xk = _dot_nt(ar, k_st)
        yield
        a_ab = jnp.where(tri_strict, xb[:rows], 0.0)
        a_rb = jnp.where(tri_incl_cat, xb[rows:], 0.0)
        a_ak = jnp.where(tri_strict, xk[:rows], 0.0)
        a_rk = jnp.where(tri_incl_cat, xk[rows:], 0.0)

        x_pow = _dot(a_ab, a_ab)
        akv = _dot(a_ak, v_st)
        yield
        t_inv = eye + a_ab
        for _ in range(n_sq - 1):
            both = _dot(jnp.concatenate([t_inv, x_pow], axis=0), x_pow)
            yield
            t_inv = t_inv + both[:rows]
            x_pow = both[rows:]
        t_inv = t_inv + _dot(t_inv, x_pow)
        yield

        s_bd = s_scr[q]
        ars = _dot_nt(ar, s_bd)
        yield
        u_st = _dot(t_inv, ars[:rows] + akv)
        yield
        y = ars[rows:] + _dot(a_rb, u_st) + _dot(a_rk, v_st)
        yield
        uv = jnp.concatenate([fold(u_st), v], axis=0)
        bk = jnp.concatenate([bv * p_end, k2 * p_end], axis=0)
        s_scr[q] = s_bd * p_c + jnp.where(same_head, _dot_tn(uv, bk), 0.0)
        yield

        y_hi, y_lo = _split_bf16(y)
        mean_hi, mean_lo = head_sums([y_hi, y_lo])
        yield
        yc = y - (mean_hi + mean_lo) * (1.0 / RWKV_HEAD)
        (var,) = head_sums([yc * yc])
        yield
        yn = yc * lax.rsqrt(var * (1.0 / RWKV_HEAD) + GN_EPS) * ln_w + ln_b
        o_ref[sl, lanes] = ((yn + bonus * v) * g_ref[sl, lanes]).astype(o_ref.dtype)

    def body(idx, carry):
        sl = pl.ds(pl.multiple_of(idx * c, c), c)
        _round_robin([chunk_step(q, sl) for q in range(n_slabs)])
        return carry

    lax.fori_loop(0, n_chunks, body, 0)

    @pl.when(tstep == pl.num_programs(2) - 1)
    def _():
        for q in range(n_slabs):
            s_bd = s_scr[q]
            out = s_bd[0:RWKV_HEAD]
            for j in range(1, HEADS_PER_SLAB):
                out = out + s_bd[j * RWKV_HEAD:(j + 1) * RWKV_HEAD]
            sf_ref[q] = out


def _wkv(r, e, k, v, a, g, prm, s0):
    b, t_in, d = r.shape
    n = RWKV_HEAD
    nslab = d // SLAB
    c = WKV_CHUNK
    t = -(-t_in // c) * c
    if t != t_in:
        r, e, k, v, a, g = (jnp.pad(z, ((0, 0), (0, t - t_in), (0, 0))) for z in (r, e, k, v, a, g))
    tb = _pow2_tile(t, 512)
    ns = WKV_SLABS_PER_STEP if nslab % WKV_SLABS_PER_STEP == 0 else 1
    s0_slab = s0.reshape(b, nslab, HEADS_PER_SLAB, n, n).transpose(0, 1, 3, 2, 4).reshape(b, nslab, n, SLAB)
    seq = pl.BlockSpec((None, tb, ns * SLAB), lambda i, s, j: (i, j, s))
    st = pl.BlockSpec((None, ns, n, SLAB), lambda i, s, j: (i, s, 0, 0))
    out, sf = pl.pallas_call(
        functools.partial(_wkv_kernel, chunk=c, n_chunks=tb // c, n_slabs=ns),
        grid=(b, nslab // ns, t // tb),
        in_specs=[seq] * 6 + [pl.BlockSpec((8, ns * SLAB), lambda i, s, j: (0, s)), st],
        out_specs=[seq, st],
        out_shape=[jax.ShapeDtypeStruct((b, t, d), BF16), jax.ShapeDtypeStruct((b, nslab, n, SLAB), F32)],
        scratch_shapes=[pltpu.VMEM((ns, SLAB, SLAB), F32)],
        compiler_params=_cparams(("parallel", "parallel", "arbitrary")),
        name="wkv7",
    )(r, e, k, v, a, g, prm, s0_slab)
    sf = sf.reshape(b, nslab, n, HEADS_PER_SLAB, n).transpose(0, 1, 3, 2, 4).reshape(b, d // n, n, n)
    return out[:, :t_in], sf


def _attn_prompt_kernel(q_ref, k_ref, v_ref, bias_ref, o_ref, kp, vp, *, t, scale):
    dh = kp.shape[1]
    zeros_past = jnp.zeros((BAND_PAST, dh), BF16)
    kp[0:BAND_PAST, :] = zeros_past
    vp[0:BAND_PAST, :] = zeros_past
    kp[BAND_PAST:, :] = k_ref[...].astype(BF16)
    vp[BAND_PAST:, :] = v_ref[...].astype(BF16)
    col = lax.broadcasted_iota(jnp.int32, (Q_BLK, KEY_WIN), 1)

    def block(c, mask_start):
        start = pl.multiple_of(c * Q_BLK, Q_BLK)
        s = _dot_nt(q_ref[pl.ds(start, Q_BLK), :], kp[pl.ds(start, KEY_WIN), :])
        yield
        s = s * scale + bias_ref[...]
        if mask_start:
            s = jnp.where(col >= BAND_PAST - c * Q_BLK, s, NEG_INF)
        m = jnp.max(s, axis=-1, keepdims=True)
        p = jnp.exp(s - m)
        p = p / jnp.sum(p, axis=-1, keepdims=True)
        o = _dot(p, vp[pl.ds(start, KEY_WIN), :])
        yield
        o_ref[pl.ds(start, Q_BLK), :] = o.astype(o_ref.dtype)

    def make_body(mask_start):
        def body(g, carry):
            _round_robin([block(g * ATTN_GROUP + i, mask_start) for i in range(ATTN_GROUP)])
            return carry
        return body

    n_grp = t // (Q_BLK * ATTN_GROUP)
    n_front = min(-(-BAND_PAST // (Q_BLK * ATTN_GROUP)), n_grp)
    lax.fori_loop(0, n_front, make_body(True), 0)
    lax.fori_loop(n_front, n_grp, make_body(False), 0)


def _attn_prompt(qkv, bias, b, t):
    d = qkv.shape[1] // 3
    h = d // ATTN_HEAD_DIM
    assert t % (ATTN_GROUP * Q_BLK) == 0
    col = lambda off: pl.BlockSpec((t, ATTN_HEAD_DIM), lambda i, j: (i, off + j))
    return pl.pallas_call(
        functools.partial(_attn_prompt_kernel, t=t, scale=ATTN_HEAD_DIM ** -0.5),
        grid=(b, h),
        in_specs=[col(0), col(h), col(2 * h),
                  pl.BlockSpec((None, Q_BLK, KEY_WIN), lambda i, j: (j, 0, 0))],
        out_specs=col(0),
        out_shape=jax.ShapeDtypeStruct((b * t, d), BF16),
        scratch_shapes=[pltpu.VMEM((BAND_PAST + t, ATTN_HEAD_DIM), BF16)] * 2,
        compiler_params=_cparams(("parallel", "parallel")),
        name="band_attn_prompt",
    )(qkv, qkv, qkv, bias)


def _attn_sample_kernel(q_ref, kn_ref, vn_ref, kc_ref, vc_ref, bias_ref, o_ref, kf, vf, *, n_keys, scale):
    rows = kc_ref.shape[0]
    t = q_ref.shape[0]
    kf[0:rows, :] = kc_ref[...].astype(BF16)
    vf[0:rows, :] = vc_ref[...].astype(BF16)
    kf[rows:rows + t, :] = kn_ref[...].astype(BF16)
    vf[rows:rows + t, :] = vn_ref[...].astype(BF16)
    pad = jnp.zeros((kf.shape[0] - rows - t, kf.shape[1]), BF16)
    kf[rows + t:, :] = pad
    vf[rows + t:, :] = pad
    s = _dot_nt(q_ref[...], kf[...]) * scale + bias_ref[...]
    col = lax.broadcasted_iota(jnp.int32, s.shape, 1)
    s = jnp.where(col < n_keys, s, NEG_INF)
    m = jnp.max(s, axis=-1, keepdims=True)
    p = jnp.exp(s - m)
    p = p / jnp.sum(p, axis=-1, keepdims=True)
    o_ref[...] = _dot(p, vf[...]).astype(o_ref.dtype)


def _attn_sample(qkv, cache_k, cache_v, bias, b, t):
    d = qkv.shape[1] // 3
    h = d // ATTN_HEAD_DIM
    rows = cache_k.shape[1]
    win = bias.shape[2]
    col = lambda off: pl.BlockSpec((t, ATTN_HEAD_DIM), lambda i, j: (i, off + j))
    cache = pl.BlockSpec((None, rows, ATTN_HEAD_DIM), lambda i, j: (i, 0, j))
    return pl.pallas_call(
        functools.partial(_attn_sample_kernel, n_keys=rows + t, scale=ATTN_HEAD_DIM ** -0.5),
        grid=(b, h),
        in_specs=[col(0), col(h), col(2 * h), cache, cache,
                  pl.BlockSpec((None, t, win), lambda i, j: (j, 0, 0))],
        out_specs=col(0),
        out_shape=jax.ShapeDtypeStruct((b * t, d), BF16),
        scratch_shapes=[pltpu.VMEM((win, ATTN_HEAD_DIM), BF16)] * 2,
        compiler_params=_cparams(("parallel", "parallel")),
        name="band_attn_sample",
    )(qkv, qkv, qkv, cache_k, cache_v, bias)


def _rel_bias(table, n_q, n_k, offset):
    span = n_q + n_k - 1
    dist = (n_q - 1) - jnp.arange(span) + offset
    rev = table[:, jnp.clip(dist, -(CHUNK - 1), REL_MAX) + (CHUNK - 1)].astype(F32)
    period = jnp.pad(rev, ((0, 0), (0, 1)))
    shifted = jnp.tile(period, (1, n_q))[:, :n_q * span].reshape(table.shape[0], n_q, span)
    return shifted[:, :, n_q - 1:n_q - 1 + n_k]


def _trunk(x, p, shift0, wkv0, cache_k, cache_v, w, is_prompt):
    b, t, d = x.shape
    m = b * t
    depth = w["ln_g"].shape[0]
    alpha = (2 * depth) ** 0.25
    xf = x.reshape(m, d)
    xb = xf.astype(BF16)
    p = p.reshape(depth, m, p.shape[-1])
    v_first = None
    shifts, wkvs, ks, vs = [], [], [], []

    def ffn(xf, xb, i, s, ln_idx):
        hidden = _ffn_up(xb, w["ffn_gate"], w["ffn_up"], (i, s))
        z = _matmul(hidden, w["ffn_down"], (i, s), F32, "ffn_down", residual=(xf, alpha, 0.5))
        return _layer_norm(z, w["ln_g"][i, ln_idx], w["ln_b"][i, ln_idx])

    for i in range(depth):
        xf, xb = ffn(xf, xb, i, 0, 0)
        j = i // 2
        if i % 2 == 0:
            x3 = xf.reshape(b, t, d)
            xr, xw, xk, xv, xa, xg = _token_mixes(x3, shift0[:, j], w["rwkv_mu"][j])
            shifts.append(x3[:, -1])
            r = _matmul(xr, w["rwkv_w_r"], (j,), F32, "rwkv_r")
            k = _matmul(xk, w["rwkv_w_k"], (j,), F32, "rwkv_k")
            v = _matmul(xv, w["rwkv_w_v"], (j,), F32, "rwkv_v")
            e = _lora(xw, *w["rwkv_w12"], w["rwkv_w0"], (j,), "decay")
            a = _lora(xa, *w["rwkv_a12"], w["rwkv_a0"], (j,), "sigmoid")
            g = _lora(xg, *w["rwkv_g12"], None, (j,), "gate")
            if v_first is None:
                v_first = v
            else:
                v = _lora(xv, *w["rwkv_v12"], w["rwkv_v0"], (j - 1,), "vmix", extra=(v, v_first))
            to3 = lambda z: z.reshape(b, t, d)
            mix, s_fin = _wkv(to3(r), to3(e), to3(k), to3(v), to3(a), to3(g), w["rwkv_prm"][j], wkv0[:, j])
            wkvs.append(s_fin)
            z = _matmul(mix.reshape(m, d), w["rwkv_w_o"], (j,), F32, "rwkv_o", residual=(xf, alpha, 1.0))
        else:
            qkv = _matmul(xb, w["attn_w_qkv"], (j,), F32, "attn_qkv")
            kv3 = qkv.reshape(b, t, 3 * d)
            h = d // ATTN_HEAD_DIM
            if is_prompt:
                rows = min(BAND_PAST, t)
                qi = jnp.arange(Q_BLK)[:, None]
                kj = jnp.arange(KEY_WIN)[None, :]
                in_band = jnp.where(qi < CHUNK, kj < BAND_PAST + CHUNK, kj >= CHUNK)
                bias = jnp.where(in_band[None], _rel_bias(w["attn_rel_bias"][j], Q_BLK, KEY_WIN, BAND_PAST), NEG_INF)
                att = _attn_prompt(qkv, bias, b, t)
                ks.append(kv3[:, t - rows:, d:2 * d].reshape(b, rows, h, ATTN_HEAD_DIM))
                vs.append(kv3[:, t - rows:, 2 * d:].reshape(b, rows, h, ATTN_HEAD_DIM))
            else:
                rows = cache_k.shape[2]
                win = -(-(rows + t) // LANES) * LANES
                att = _attn_sample(qkv, cache_k[:, j].reshape(b, rows, d), cache_v[:, j].reshape(b, rows, d),
                                   _rel_bias(w["attn_rel_bias"][j], t, win, rows), b, t)
                ks.append(kv3[:, :, d:2 * d].reshape(b, t, h, ATTN_HEAD_DIM))
                vs.append(kv3[:, :, 2 * d:].reshape(b, t, h, ATTN_HEAD_DIM))
            z = _matmul(att, w["attn_w_o"], (j,), F32, "attn_o", residual=(xf, alpha, 1.0))
        xf, xb = _layer_norm(z, w["ln_g"][i, 1], w["ln_b"][i, 1])
        xf, xb = ffn(xf, xb, i, 1, 2)
        xf, xb = _ple(xf, xb, p, w["ple_w_gate"], w["ple_w_proj"], (i,))
    return (xf.reshape(b, t, d), jnp.stack(shifts, axis=1), jnp.stack(wkvs, axis=1),
            jnp.stack(ks, axis=1), jnp.stack(vs, axis=1))


def _prepare_weights(ln_g, ln_b, ffn_w_gate, ffn_w_up, ffn_w_down, ple_w_gate, ple_w_proj,
                     rwkv_mu, rwkv_w_r, rwkv_w_k, rwkv_w_v, rwkv_w_o, rwkv_w0, rwkv_w1, rwkv_w2,
                     rwkv_a0, rwkv_a1, rwkv_a2, rwkv_v0, rwkv_v1, rwkv_v2, rwkv_g1, rwkv_g2,
                     rwkv_k_k, rwkv_k_a, rwkv_r_k, rwkv_ln_w, rwkv_ln_b,
                     attn_w_qkv, attn_w_o, attn_rel_bias):
    n_rwkv = rwkv_mu.shape[0]
    d = ln_g.shape[-1]
    cast = lambda a: a.astype(BF16)
    row = lambda a: a.reshape(a.shape[0], 1, d)
    zeros = jnp.zeros((n_rwkv, d), F32)
    return dict(
        ln_g=ln_g, ln_b=ln_b,
        ffn_gate=cast(ffn_w_gate), ffn_up=cast(ffn_w_up), ffn_down=cast(ffn_w_down),
        ple_w_gate=cast(ple_w_gate), ple_w_proj=cast(ple_w_proj),
        rwkv_mu=rwkv_mu,
        rwkv_w_r=cast(rwkv_w_r), rwkv_w_k=cast(rwkv_w_k), rwkv_w_v=cast(rwkv_w_v), rwkv_w_o=cast(rwkv_w_o),
        rwkv_w0=row(rwkv_w0), rwkv_a0=row(rwkv_a0), rwkv_v0=row(rwkv_v0),
        rwkv_w12=_pad_lora(rwkv_w1, rwkv_w2), rwkv_a12=_pad_lora(rwkv_a1, rwkv_a2),
        rwkv_v12=_pad_lora(rwkv_v1, rwkv_v2), rwkv_g12=_pad_lora(rwkv_g1, rwkv_g2),
        rwkv_prm=jnp.stack([rwkv_k_k, rwkv_k_a, rwkv_r_k.reshape(n_rwkv, d), rwkv_ln_w, rwkv_ln_b,
                            zeros, zeros, zeros], axis=1),
        attn_w_qkv=cast(attn_w_qkv), attn_w_o=cast(attn_w_o), attn_rel_bias=attn_rel_bias,
    )


def kernel(x_prompt, x_sample, state_shift, state_wkv, cache_k, cache_v, p_prompt, p_sample,
           ln_g, ln_b, ffn_w_gate, ffn_w_up, ffn_w_down, ple_w_gate, ple_w_proj,
           rwkv_mu, rwkv_w_r, rwkv_w_k, rwkv_w_v, rwkv_w_o, rwkv_w0, rwkv_w1, rwkv_w2,
           rwkv_a0, rwkv_a1, rwkv_a2, rwkv_v0, rwkv_v1, rwkv_v2, rwkv_g1, rwkv_g2,
           rwkv_k_k, rwkv_k_a, rwkv_r_k, rwkv_ln_w, rwkv_ln_b,
           attn_w_qkv, attn_w_o, attn_rel_bias):
    w = _prepare_weights(ln_g, ln_b, ffn_w_gate, ffn_w_up, ffn_w_down, ple_w_gate, ple_w_proj,
                         rwkv_mu, rwkv_w_r, rwkv_w_k, rwkv_w_v, rwkv_w_o, rwkv_w0, rwkv_w1, rwkv_w2,
                         rwkv_a0, rwkv_a1, rwkv_a2, rwkv_v0, rwkv_v1, rwkv_v2, rwkv_g1, rwkv_g2,
                         rwkv_k_k, rwkv_k_a, rwkv_r_k, rwkv_ln_w, rwkv_ln_b,
                         attn_w_qkv, attn_w_o, attn_rel_bias)
    bp, _, d = x_prompt.shape
    n_rwkv = rwkv_mu.shape[0]
    heads = d // RWKV_HEAD
    zero_shift = jnp.zeros((bp, n_rwkv, d), x_prompt.dtype)
    zero_wkv = jnp.zeros((bp, n_rwkv, heads, RWKV_HEAD, RWKV_HEAD), x_prompt.dtype)
    y_p, shift_p, wkv_p, k_p, v_p = _trunk(x_prompt, p_prompt, zero_shift, zero_wkv, None, None, w, True)
    y_s, shift_s, wkv_s, k_s, v_s = _trunk(x_sample, p_sample, state_shift, state_wkv, cache_k, cache_v, w, False)
    return (y_p, y_s, shift_p, wkv_p, k_p, v_p, shift_s, wkv_s, k_s, v_s)
```

```python
import functools
import math

import jax
import jax.numpy as jnp
from jax import lax
from jax.experimental import pallas as pl
from jax.experimental.pallas import tpu as pltpu

F32 = jnp.float32
BF16 = jnp.bfloat16

CHUNK = 64
BAND_CHUNKS = 8
BAND_PAST = BAND_CHUNKS * CHUNK
REL_MAX = 256
ATTN_HEAD_DIM = 128
RWKV_HEAD = 64
LN_EPS = 1e-5
GN_EPS = 64e-5
NEG_INF = -1e30

LANES = 128
SLAB = 256
HEADS_PER_SLAB = SLAB // RWKV_HEAD
WKV_CHUNK = 64
WKV_SLABS_PER_STEP = 8
WKV_BLOCK_ELEMS = 512 * 1024
Q_BLK = 2 * CHUNK
KEY_WIN = BAND_PAST + Q_BLK
ATTN_GROUP = 4
VMEM_LIMIT = 56 * 1024 * 1024
MM_VMEM_BUDGET = 50 * 1024 * 1024


def _cparams(sem):
    return pltpu.CompilerParams(dimension_semantics=sem, vmem_limit_bytes=VMEM_LIMIT)


def _pow2_tile(m, cap):
    if m <= cap:
        return m
    t = 1 << (cap.bit_length() - 1)
    while m % t:
        t //= 2
    return t


def _lane_tile(n, cap):
    t = max(LANES, (cap // LANES) * LANES)
    while n % t:
        t -= LANES
    return t


def _dot(a, b):
    return jnp.dot(a.astype(BF16), b.astype(BF16), preferred_element_type=F32)


def _dot_nt(a, b):
    return lax.dot_general(a.astype(BF16), b.astype(BF16), (((1,), (1,)), ((), ())),
                           preferred_element_type=F32)


def _dot_tn(a, b):
    return lax.dot_general(a.astype(BF16), b.astype(BF16), (((0,), (0,)), ((), ())),
                           preferred_element_type=F32)


def _mm_kernel(x_ref, w_ref, o_ref):
    o_ref[...] = jnp.dot(x_ref[...], w_ref[...], preferred_element_type=F32).astype(o_ref.dtype)


def _mm_tiles(m, k, n, n_f32_blocks):
    tm = _pow2_tile(m, max(8, (12 << 20) // (2 * k)))
    tn = n
    while tn > LANES and (n % tn or 2 * (2 * tm * k + 2 * k * tn + 4 * n_f32_blocks * tm * tn) > MM_VMEM_BUDGET):
        tn -= LANES
    return tm, tn


def _wspec(w, lead, k, tn):
    assert w.ndim == len(lead) + 2
    return pl.BlockSpec((None,) * len(lead) + (k, tn), lambda i, j: tuple(lead) + (0, j))


def _mm_residual_kernel(x_ref, w_ref, r_ref, o_ref, *, alpha, scale):
    o_ref[...] = alpha * r_ref[...] + scale * jnp.dot(x_ref[...], w_ref[...], preferred_element_type=F32)


def _matmul(x, w, lead, out_dtype, name, residual=None):
    m, k = x.shape
    n = w.shape[-1]
    tm, tn = _mm_tiles(m, k, n, 1 if residual is None else 2)
    out_blk = pl.BlockSpec((tm, tn), lambda i, j: (i, j))
    in_specs = [pl.BlockSpec((tm, k), lambda i, j: (i, 0)), _wspec(w, lead, k, tn)]
    if residual is None:
        body, args = _mm_kernel, (x, w)
    else:
        r, alpha, scale = residual
        body, args = functools.partial(_mm_residual_kernel, alpha=alpha, scale=scale), (x, w, r)
        in_specs.append(out_blk)
    return pl.pallas_call(
        body,
        grid=(m // tm, n // tn),
        in_specs=in_specs,
        out_specs=out_blk,
        out_shape=jax.ShapeDtypeStruct((m, n), out_dtype),
        compiler_params=_cparams(("parallel", "arbitrary")),
        name=name,
    )(*args)


def _ffn_up_kernel(x_ref, wg_ref, wu_ref, o_ref):
    x = x_ref[...]
    g = jnp.dot(x, wg_ref[...], preferred_element_type=F32)
    u = jnp.dot(x, wu_ref[...], preferred_element_type=F32)
    o_ref[...] = (g * jax.nn.sigmoid(g) * u).astype(o_ref.dtype)


def _ffn_up(xb, wg, wu, lead):
    m, k = xb.shape
    n = wg.shape[-1]
    tm = _pow2_tile(m, max(8, (16 << 20) // (2 * k)))
    tn = _lane_tile(n, max(LANES, (2 << 20) // (2 * k)))
    return pl.pallas_call(
        _ffn_up_kernel,
        grid=(m // tm, n // tn),
        in_specs=[pl.BlockSpec((tm, k), lambda i, j: (i, 0)), _wspec(wg, lead, k, tn), _wspec(wu, lead, k, tn)],
        out_specs=pl.BlockSpec((tm, tn), lambda i, j: (i, j)),
        out_shape=jax.ShapeDtypeStruct((m, n), BF16),
        compiler_params=_cparams(("parallel", "arbitrary")),
        name="ffn_up",
    )(xb, wg, wu)


def _ln_kernel(z_ref, g_ref, b_ref, of_ref, ob_ref):
    z = z_ref[...]
    mu = jnp.mean(z, axis=-1, keepdims=True)
    zc = z - mu
    var = jnp.mean(zc * zc, axis=-1, keepdims=True)
    o = zc * lax.rsqrt(var + LN_EPS) * g_ref[...] + b_ref[...]
    of_ref[...] = o
    ob_ref[...] = o.astype(BF16)


def _layer_norm(z, g, b):
    m, d = z.shape
    tr = _pow2_tile(m, max(8, (4 << 20) // (4 * d)))
    row = pl.BlockSpec((tr, d), lambda i: (i, 0))
    vec = pl.BlockSpec((1, d), lambda i: (0, 0))
    return pl.pallas_call(
        _ln_kernel,
        grid=(m // tr,),
        in_specs=[row, vec, vec],
        out_specs=[row, row],
        out_shape=[jax.ShapeDtypeStruct((m, d), F32), jax.ShapeDtypeStruct((m, d), BF16)],
        compiler_params=_cparams(("parallel",)),
        name="layer_norm",
    )(z, g.reshape(1, d), b.reshape(1, d))


def _ple_kernel(xb_ref, wg_ref, p_ref, wp_ref, x_ref, of_ref, ob_ref):
    gate = jax.nn.sigmoid(jnp.dot(xb_ref[...], wg_ref[...], preferred_element_type=F32))
    emb = jnp.dot(p_ref[...].astype(BF16), wp_ref[...], preferred_element_type=F32)
    o = x_ref[...] + gate * emb
    of_ref[...] = o
    ob_ref[...] = o.astype(BF16)


def _ple(x, xb, p, wg, wp, lead):
    m, d = x.shape
    pd = p.shape[-1]
    tm = _pow2_tile(m, max(8, (8 << 20) // (2 * d)))
    tn = _lane_tile(d, max(LANES, (4 << 20) // (2 * d)))
    blk = pl.BlockSpec((tm, tn), lambda i, j: (i, j))
    return pl.pallas_call(
        _ple_kernel,
        grid=(m // tm, d // tn),
        in_specs=[pl.BlockSpec((tm, d), lambda i, j: (i, 0)), _wspec(wg, lead, d, tn),
                  pl.BlockSpec((None, tm, pd), lambda i, j: tuple(lead) + (i, 0)), _wspec(wp, lead, pd, tn), blk],
        out_specs=[blk, blk],
        out_shape=[jax.ShapeDtypeStruct((m, d), F32), jax.ShapeDtypeStruct((m, d), BF16)],
        compiler_params=_cparams(("parallel", "arbitrary")),
        name="ple",
    )(xb, wg, p, wp, x)


def _mix_kernel(x_ref, prev_ref, mu_ref, *o_refs):
    x = x_ref[...]
    shifted = pltpu.roll(x, 1, 0)
    first = lax.broadcasted_iota(jnp.int32, x.shape, 0) == 0
    shifted = jnp.where(first, prev_ref[...], shifted)
    xx = shifted - x
    for m, o_ref in enumerate(o_refs):
        o_ref[...] = (x + xx * mu_ref[m:m + 1, :]).astype(BF16)


def _token_mixes(x, x_prev, mu):
    b, t, d = x.shape
    tt = _pow2_tile(t, max(8, (4 << 20) // (4 * d)))
    nt = t // tt
    prev = jnp.concatenate([x_prev[:, None, :], x[:, tt - 1:t - 1:tt, :]], axis=1).reshape(b, nt, 1, d)
    blk = pl.BlockSpec((None, tt, d), lambda i, j: (i, j, 0))
    outs = pl.pallas_call(
        _mix_kernel,
        grid=(b, nt),
        in_specs=[blk,
                  pl.BlockSpec((None, None, 1, d), lambda i, j: (i, j, 0, 0)),
                  pl.BlockSpec((6, d), lambda i, j: (0, 0))],
        out_specs=[blk] * 6,
        out_shape=[jax.ShapeDtypeStruct((b, t, d), BF16)] * 6,
        compiler_params=_cparams(("parallel", "parallel")),
        name="token_mix",
    )(x, prev, mu)
    return [o.reshape(b * t, d) for o in outs]


def _lora_kernel(x_ref, w1_ref, w2_ref, *rest, mode):
    t = jnp.dot(x_ref[...], w1_ref[...], preferred_element_type=F32)
    if mode == "decay":
        t = jnp.tanh(t)
    elif mode == "gate":
        t = jax.nn.sigmoid(t)
    z = jnp.dot(t.astype(BF16), w2_ref[...], preferred_element_type=F32)
    if mode == "gate":
        (o_ref,) = rest
        o_ref[...] = z
    elif mode == "decay":
        bias_ref, o_ref = rest
        u = -(bias_ref[...] + z)
        softplus = jnp.maximum(u, 0.0) + jnp.log(1.0 + jnp.exp(-jnp.abs(u)))
        o_ref[...] = jnp.exp(-softplus - 0.5)
    elif mode == "sigmoid":
        bias_ref, o_ref = rest
        o_ref[...] = jax.nn.sigmoid(bias_ref[...] + z)
    else:
        bias_ref, v_ref, vf_ref, o_ref = rest
        v = v_ref[...]
        o_ref[...] = v + (vf_ref[...] - v) * jax.nn.sigmoid(bias_ref[...] + z)


def _lora(xb, w1, w2, bias, lead, mode, extra=()):
    m, d = xb.shape
    r = w1.shape[-1]
    tm = _pow2_tile(m, 256)
    row = pl.BlockSpec((tm, d), lambda i: (i, 0))
    whole = lambda a, b: pl.BlockSpec((None, a, b), lambda i: tuple(lead) + (0, 0))
    assert (bias is None) == (mode == "gate")
    biases = () if bias is None else (bias,)
    return pl.pallas_call(
        functools.partial(_lora_kernel, mode=mode),
        grid=(m // tm,),
        in_specs=[row, whole(d, r), whole(r, d)] + [whole(1, d)] * len(biases) + [row] * len(extra),
        out_specs=row,
        out_shape=jax.ShapeDtypeStruct((m, d), F32),
        compiler_params=_cparams(("parallel",)),
        name="lora_" + mode,
    )(xb, w1, w2, *biases, *extra)


def _pad_lora(w1, w2):
    r = w1.shape[-1]
    rp = -(-r // LANES) * LANES
    w1 = jnp.pad(w1.astype(BF16), ((0, 0), (0, 0), (0, rp - r)))
    w2 = jnp.pad(w2.astype(BF16), ((0, 0), (0, rp - r), (0, 0)))
    return w1, w2


def _round_robin(gens, batched=None):
    replies = [None] * len(gens)
    while gens:
        alive, asks = [], []
        for g, reply in zip(gens, replies):
            try:
                asks.append(g.send(reply))
                alive.append(g)
            except StopIteration:
                pass
        gens = alive
        replies = batched(asks) if asks and asks[0] is not None else [None] * len(gens)


def _split_bf16(x):
    hi = x.astype(BF16)
    lo = (x - hi.astype(F32)).astype(BF16)
    return hi, lo


def _wkv_kernel(r_ref, e_ref, k_ref, v_ref, a_ref, g_ref, prm_ref, s0_ref, o_ref, sf_ref, s_scr,
                *, chunk, n_chunks, n_slabs):
    c = chunk
    rows = HEADS_PER_SLAB * c
    tstep = pl.program_id(2)

    lane = lax.broadcasted_iota(jnp.int32, (1, SLAB), 1)
    head_masks = [(lane // RWKV_HEAD) == j for j in range(HEADS_PER_SLAB)]
    ri = lax.broadcasted_iota(jnp.int32, (rows, rows), 0)
    ci = lax.broadcasted_iota(jnp.int32, (rows, rows), 1)
    tri_strict = (ri % c) > (ci % c)
    tri_incl_cat = (lax.broadcasted_iota(jnp.int32, (c, rows), 0)
                    >= lax.broadcasted_iota(jnp.int32, (c, rows), 1) % c)
    eye = jnp.where(ri == ci, 1.0, 0.0).astype(F32)
    rs = lax.broadcasted_iota(jnp.int32, (SLAB, SLAB), 0)
    cs = lax.broadcasted_iota(jnp.int32, (SLAB, SLAB), 1)
    same_head = (rs // RWKV_HEAD) == (cs // RWKV_HEAD)
    ones_bd = jnp.where(same_head, 1.0, 0.0).astype(BF16)
    lt_incl = jnp.where(lax.broadcasted_iota(jnp.int32, (c, c), 0) >= lax.broadcasted_iota(jnp.int32, (c, c), 1),
                        1.0, 0.0).astype(BF16)

    @pl.when(tstep == 0)
    def _():
        for q in range(n_slabs):
            s0 = s0_ref[q]
            s_scr[q] = jnp.where(same_head, jnp.concatenate([s0] * HEADS_PER_SLAB, axis=0), 0.0)

    def stack(x):
        return jnp.concatenate([jnp.where(m, x, 0.0) for m in head_masks], axis=0)

    def fold(y):
        out = y[0:c]
        for j in range(1, HEADS_PER_SLAB):
            out = out + y[j * c:(j + 1) * c]
        return out

    def head_sums(asks):
        flat = [x.astype(BF16) for xs in asks for x in xs]
        tot = jnp.dot(jnp.concatenate(flat, axis=0), ones_bd, preferred_element_type=F32)
        parts = iter(tot[i * c:(i + 1) * c] for i in range(len(flat)))
        return [[next(parts) for _ in xs] for xs in asks]

    n_sq = int(math.log2(c)) - 1

    def chunk_step(q, sl):
        lanes = slice(q * SLAB, (q + 1) * SLAB)
        prm = prm_ref[:, lanes]
        k_k, k_a, r_k, ln_w, ln_b = (prm[i:i + 1, :] for i in range(5))
        r = r_ref[sl, lanes]
        e = e_ref[sl, lanes]
        k = k_ref[sl, lanes]
        v = v_ref[sl, lanes]
        a = a_ref[sl, lanes]

        kk = k * k_k
        k2 = k * (1.0 + (a - 1.0) * k_a)
        ss, bonus = yield [kk * kk, r * k2 * r_k]
        kkn = kk / jnp.maximum(jnp.sqrt(ss), 1e-12)
        av = -kkn
        bv = kkn * a

        e_hi, e_lo = _split_bf16(e)
        cum = (jnp.dot(lt_incl, e_hi, preferred_element_type=F32)
               + jnp.dot(lt_incl, e_lo, preferred_element_type=F32))
        yield
        cum_last = cum[c - 1:c, :]
        p_in = jnp.exp(-cum)
        p_ex = jnp.exp(e - cum)
        p_inv = jnp.exp(cum)
        p_end = jnp.exp(cum - cum_last)
        p_c = jnp.exp(-cum_last)

        b_st = stack(bv * p_inv)
        k_st = stack(k2 * p_inv)
        v_st = stack(v)
        ar = jnp.concatenate([stack(av * p_ex), r * p_in], axis=0).astype(BF16)
        xb = _dot_nt(ar, b_st)
        yield
        xk = _dot_nt(ar, k_st)
        yield
        a_ab = jnp.where(tri_strict, xb[:rows], 0.0)
        a_rb = jnp.where(tri_incl_cat, xb[rows:], 0.0)
        a_ak = jnp.where(tri_strict, xk[:rows], 0.0)
        a_rk = jnp.where(tri_incl_cat, xk[rows:], 0.0)

        x_pow = _dot(a_ab, a_ab)
        akv = _dot(a_ak, v_st)
        yield
        t_inv = eye + a_ab
        for _ in range(n_sq - 1):
            both = _dot(jnp.concatenate([t_inv, x_pow], axis=0), x_pow)
            yield
            t_inv = t_inv + both[:rows]
            x_pow = both[rows:]
        t_inv = t_inv + _dot(t_inv, x_pow)
        yield

        s_bd = s_scr[q]
        ars = _dot_nt(ar, s_bd)
        yield
        u_st = _dot(t_inv, ars[:rows] + akv)
        yield
        y = ars[rows:] + _dot(a_rb, u_st) + _dot(a_rk, v_st)
        yield
        uv = jnp.concatenate([fold(u_st), v], axis=0)
        bk = jnp.concatenate([bv * p_end, k2 * p_end], axis=0)
        s_scr[q] = s_bd * p_c + jnp.where(same_head, _dot_tn(uv, bk), 0.0)
        yield

        mean_hi, mean_lo = yield list(_split_bf16(y))
        yc = y - (mean_hi + mean_lo) * (1.0 / RWKV_HEAD)
        (var,) = yield [yc * yc]
        yn = yc * lax.rsqrt(var * (1.0 / RWKV_HEAD) + GN_EPS) * ln_w + ln_b
        o_ref[sl, lanes] = ((yn + bonus * v) * g_ref[sl, lanes]).astype(o_ref.dtype)

    def body(idx, carry):
        sl = pl.ds(pl.multiple_of(idx * c, c), c)
        _round_robin([chunk_step(q, sl) for q in range(n_slabs)], batched=head_sums)
        return carry

    lax.fori_loop(0, n_chunks, body, 0)

    @pl.when(tstep == pl.num_programs(2) - 1)
    def _():
        for q in range(n_slabs):
            s_bd = s_scr[q]
            out = s_bd[0:RWKV_HEAD]
            for j in range(1, HEADS_PER_SLAB):
                out = out + s_bd[j * RWKV_HEAD:(j + 1) * RWKV_HEAD]
            sf_ref[q] = out


def _wkv(r, e, k, v, a, g, prm, s0):
    b, t_in, d = r.shape
    n = RWKV_HEAD
    nslab = d // SLAB
    c = WKV_CHUNK
    t = -(-t_in // c) * c
    if t != t_in:
        r, e, k, v, a, g = (jnp.pad(z, ((0, 0), (0, t - t_in), (0, 0))) for z in (r, e, k, v, a, g))
    ns = max(s for s in (1, 2, 4, 8) if s <= WKV_SLABS_PER_STEP and nslab % s == 0)
    tb = _pow2_tile(t, max(c, WKV_BLOCK_ELEMS // (ns * SLAB)))
    s0_slab = s0.reshape(b, nslab, HEADS_PER_SLAB, n, n).transpose(0, 1, 3, 2, 4).reshape(b, nslab, n, SLAB)
    seq = pl.BlockSpec((None, tb, ns * SLAB), lambda i, s, j: (i, j, s))
    st = pl.BlockSpec((None, ns, n, SLAB), lambda i, s, j: (i, s, 0, 0))
    out, sf = pl.pallas_call(
        functools.partial(_wkv_kernel, chunk=c, n_chunks=tb // c, n_slabs=ns),
        grid=(b, nslab // ns, t // tb),
        in_specs=[seq] * 6 + [pl.BlockSpec((8, ns * SLAB), lambda i, s, j: (0, s)), st],
        out_specs=[seq, st],
        out_shape=[jax.ShapeDtypeStruct((b, t, d), BF16), jax.ShapeDtypeStruct((b, nslab, n, SLAB), F32)],
        scratch_shapes=[pltpu.VMEM((ns, SLAB, SLAB), F32)],
        compiler_params=_cparams(("parallel", "parallel", "arbitrary")),
        name="wkv7",
    )(r, e, k, v, a, g, prm, s0_slab)
    sf = sf.reshape(b, nslab, n, HEADS_PER_SLAB, n).transpose(0, 1, 3, 2, 4).reshape(b, d // n, n, n)
    return out[:, :t_in], sf


def _attn_prompt_kernel(q_ref, k_ref, v_ref, bias_ref, o_ref, kp, vp, *, t, scale):
    dh = kp.shape[1]
    zeros_past = jnp.zeros((BAND_PAST, dh), BF16)
    kp[0:BAND_PAST, :] = zeros_past
    vp[0:BAND_PAST, :] = zeros_past
    kp[BAND_PAST:, :] = k_ref[...].astype(BF16)
    vp[BAND_PAST:, :] = v_ref[...].astype(BF16)
    col = lax.broadcasted_iota(jnp.int32, (Q_BLK, KEY_WIN), 1)

    def block(c, mask_start):
        start = pl.multiple_of(c * Q_BLK, Q_BLK)
        s = _dot_nt(q_ref[pl.ds(start, Q_BLK), :], kp[pl.ds(start, KEY_WIN), :])
        yield
        s = s * scale + bias_ref[...]
        if mask_start:
            s = jnp.where(col >= BAND_PAST - c * Q_BLK, s, NEG_INF)
        m = jnp.max(s, axis=-1, keepdims=True)
        p = jnp.exp(s - m)
        p = p / jnp.sum(p, axis=-1, keepdims=True)
        o = _dot(p, vp[pl.ds(start, KEY_WIN), :])
        yield
        o_ref[pl.ds(start, Q_BLK), :] = o.astype(o_ref.dtype)

    def make_body(mask_start):
        def body(g, carry):
            _round_robin([block(g * ATTN_GROUP + i, mask_start) for i in range(ATTN_GROUP)])
            return carry
        return body

    n_grp = t // (Q_BLK * ATTN_GROUP)
    n_front = min(-(-BAND_PAST // (Q_BLK * ATTN_GROUP)), n_grp)
    lax.fori_loop(0, n_front, make_body(True), 0)
    lax.fori_loop(n_front, n_grp, make_body(False), 0)


def _attn_prompt(qkv, bias, b, t):
    d = qkv.shape[1] // 3
    h = d // ATTN_HEAD_DIM
    assert t % (ATTN_GROUP * Q_BLK) == 0
    col = lambda off: pl.BlockSpec((t, ATTN_HEAD_DIM), lambda i, j: (i, off + j))
    return pl.pallas_call(
        functools.partial(_attn_prompt_kernel, t=t, scale=ATTN_HEAD_DIM ** -0.5),
        grid=(b, h),
        in_specs=[col(0), col(h), col(2 * h),
                  pl.BlockSpec((None, Q_BLK, KEY_WIN), lambda i, j: (j, 0, 0))],
        out_specs=col(0),
        out_shape=jax.ShapeDtypeStruct((b * t, d), BF16),
        scratch_shapes=[pltpu.VMEM((BAND_PAST + t, ATTN_HEAD_DIM), BF16)] * 2,
        compiler_params=_cparams(("parallel", "parallel")),
        name="band_attn_prompt",
    )(qkv, qkv, qkv, bias)


def _attn_sample_kernel(q_ref, kn_ref, vn_ref, kc_ref, vc_ref, bias_ref, o_ref, kf, vf, *, n_keys, scale, heads):
    rows = kc_ref.shape[0]
    t = q_ref.shape[0]
    dh = ATTN_HEAD_DIM
    pad = jnp.zeros((kf.shape[1] - rows - t, dh), BF16)

    def head(h):
        ln = slice(h * dh, (h + 1) * dh)
        for buf, cache_ref, new_ref in ((kf, kc_ref, kn_ref), (vf, vc_ref, vn_ref)):
            buf[h, 0:rows, :] = cache_ref[:, ln].astype(BF16)
            buf[h, rows:rows + t, :] = new_ref[:, ln].astype(BF16)
            buf[h, rows + t:, :] = pad
        s = _dot_nt(q_ref[:, ln], kf[h])
        yield
        s = s * scale + bias_ref[h]
        col = lax.broadcasted_iota(jnp.int32, s.shape, 1)
        s = jnp.where(col < n_keys, s, NEG_INF)
        m = jnp.max(s, axis=-1, keepdims=True)
        p = jnp.exp(s - m)
        p = p / jnp.sum(p, axis=-1, keepdims=True)
        o = _dot(p, vf[h])
        yield
        o_ref[:, ln] = o.astype(o_ref.dtype)

    _round_robin([head(h) for h in range(heads)])


def _attn_sample(qkv, cache_k, cache_v, layer, bias, b, t):
    d = qkv.shape[1] // 3
    h = d // ATTN_HEAD_DIM
    rows = cache_k.shape[2]
    win = bias.shape[2]
    hs = max(s for s in (1, 2, 4) if h % s == 0)
    width = hs * ATTN_HEAD_DIM
    col = lambda off: pl.BlockSpec((t, width), lambda i, j: (i, off + j))
    cache = pl.BlockSpec((None, None, rows, width), lambda i, j: (i, layer, 0, j))
    return pl.pallas_call(
        functools.partial(_attn_sample_kernel, n_keys=rows + t, scale=ATTN_HEAD_DIM ** -0.5, heads=hs),
        grid=(b, h // hs),
        in_specs=[col(0), col(h // hs), col(2 * (h // hs)), cache, cache,
                  pl.BlockSpec((hs, t, win), lambda i, j: (j, 0, 0))],
        out_specs=col(0),
        out_shape=jax.ShapeDtypeStruct((b * t, d), BF16),
        scratch_shapes=[pltpu.VMEM((hs, win, ATTN_HEAD_DIM), BF16)] * 2,
        compiler_params=_cparams(("parallel", "parallel")),
        name="band_attn_sample",
    )(qkv, qkv, qkv, cache_k, cache_v, bias)


def _rel_bias(table, n_q, n_k, offset):
    span = n_q + n_k - 1
    dist = (n_q - 1) - jnp.arange(span) + offset
    rev = table[:, jnp.clip(dist, -(CHUNK - 1), REL_MAX) + (CHUNK - 1)].astype(F32)
    period = jnp.pad(rev, ((0, 0), (0, 1)))
    shifted = jnp.tile(period, (1, n_q))[:, :n_q * span].reshape(table.shape[0], n_q, span)
    return shifted[:, :, n_q - 1:n_q - 1 + n_k]


def _trunk(x, p, shift0, wkv0, cache_k, cache_v, w, is_prompt):
    b, t, d = x.shape
    m = b * t
    depth = w["ln_g"].shape[0]
    alpha = (2 * depth) ** 0.25
    xf = x.reshape(m, d)
    xb = xf.astype(BF16)
    p = p.reshape(depth, m, p.shape[-1])
    v_first = None
    shifts, wkvs, ks, vs = [], [], [], []

    def ffn(xf, xb, i, s, ln_idx):
        hidden = _ffn_up(xb, w["ffn_gate"], w["ffn_up"], (i, s))
        z = _matmul(hidden, w["ffn_down"], (i, s), F32, "ffn_down", residual=(xf, alpha, 0.5))
        return _layer_norm(z, w["ln_g"][i, ln_idx], w["ln_b"][i, ln_idx])

    for i in range(depth):
        xf, xb = ffn(xf, xb, i, 0, 0)
        j = i // 2
        if i % 2 == 0:
            x3 = xf.reshape(b, t, d)
            xr, xw, xk, xv, xa, xg = _token_mixes(x3, shift0[:, j], w["rwkv_mu"][j])
            shifts.append(x3[:, -1])
            r = _matmul(xr, w["rwkv_w_r"], (j,), F32, "rwkv_r")
            k = _matmul(xk, w["rwkv_w_k"], (j,), F32, "rwkv_k")
            v = _matmul(xv, w["rwkv_w_v"], (j,), F32, "rwkv_v")
            e = _lora(xw, *w["rwkv_w12"], w["rwkv_w0"], (j,), "decay")
            a = _lora(xa, *w["rwkv_a12"], w["rwkv_a0"], (j,), "sigmoid")
            g = _lora(xg, *w["rwkv_g12"], None, (j,), "gate")
            if v_first is None:
                v_first = v
            else:
                v = _lora(xv, *w["rwkv_v12"], w["rwkv_v0"], (j - 1,), "vmix", extra=(v, v_first))
            to3 = lambda z: z.reshape(b, t, d)
            mix, s_fin = _wkv(to3(r), to3(e), to3(k), to3(v), to3(a), to3(g), w["rwkv_prm"][j], wkv0[:, j])
            wkvs.append(s_fin)
            z = _matmul(mix.reshape(m, d), w["rwkv_w_o"], (j,), F32, "rwkv_o", residual=(xf, alpha, 1.0))
        else:
            qkv = _matmul(xb, w["attn_w_qkv"], (j,), F32, "attn_qkv")
            kv3 = qkv.reshape(b, t, 3 * d)
            h = d // ATTN_HEAD_DIM
            if is_prompt:
                rows = min(BAND_PAST, t)
                qi = jnp.arange(Q_BLK)[:, None]
                kj = jnp.arange(KEY_WIN)[None, :]
                in_band = jnp.where(qi < CHUNK, kj < BAND_PAST + CHUNK, kj >= CHUNK)
                bias = jnp.where(in_band[None], _rel_bias(w["attn_rel_bias"][j], Q_BLK, KEY_WIN, BAND_PAST), NEG_INF)
                att = _attn_prompt(qkv, bias, b, t)
                ks.append(kv3[:, t - rows:, d:2 * d].reshape(b, rows, h, ATTN_HEAD_DIM))
                vs.append(kv3[:, t - rows:, 2 * d:].reshape(b, rows, h, ATTN_HEAD_DIM))
            else:
                rows = cache_k.shape[2]
                win = -(-(rows + t) // LANES) * LANES
                n_att = cache_k.shape[1]
                att = _attn_sample(qkv, cache_k.reshape(b, n_att, rows, d), cache_v.reshape(b, n_att, rows, d), j,
                                   _rel_bias(w["attn_rel_bias"][j], t, win, rows), b, t)
                ks.append(kv3[:, :, d:2 * d].reshape(b, t, h, ATTN_HEAD_DIM))
                vs.append(kv3[:, :, 2 * d:].reshape(b, t, h, ATTN_HEAD_DIM))
            z = _matmul(att, w["attn_w_o"], (j,), F32, "attn_o", residual=(xf, alpha, 1.0))
        xf, xb = _layer_norm(z, w["ln_g"][i, 1], w["ln_b"][i, 1])
        xf, xb = ffn(xf, xb, i, 1, 2)
        xf, xb = _ple(xf, xb, p, w["ple_w_gate"], w["ple_w_proj"], (i,))
    return (xf.reshape(b, t, d), jnp.stack(shifts, axis=1), jnp.stack(wkvs, axis=1),
            jnp.stack(ks, axis=1), jnp.stack(vs, axis=1))


def _prepare_weights(ln_g, ln_b, ffn_w_gate, ffn_w_up, ffn_w_down, ple_w_gate, ple_w_proj,
                     rwkv_mu, rwkv_w_r, rwkv_w_k, rwkv_w_v, rwkv_w_o, rwkv_w0, rwkv_w1, rwkv_w2,
                     rwkv_a0, rwkv_a1, rwkv_a2, rwkv_v0, rwkv_v1, rwkv_v2, rwkv_g1, rwkv_g2,
                     rwkv_k_k, rwkv_k_a, rwkv_r_k, rwkv_ln_w, rwkv_ln_b,
                     attn_w_qkv, attn_w_o, attn_rel_bias):
    n_rwkv = rwkv_mu.shape[0]
    d = ln_g.shape[-1]
    cast = lambda a: a.astype(BF16)
    row = lambda a: a.reshape(a.shape[0], 1, d)
    zeros = jnp.zeros((n_rwkv, d), F32)
    return dict(
        ln_g=ln_g, ln_b=ln_b,
        ffn_gate=cast(ffn_w_gate), ffn_up=cast(ffn_w_up), ffn_down=cast(ffn_w_down),
        ple_w_gate=cast(ple_w_gate), ple_w_proj=cast(ple_w_proj),
        rwkv_mu=rwkv_mu,
        rwkv_w_r=cast(rwkv_w_r), rwkv_w_k=cast(rwkv_w_k), rwkv_w_v=cast(rwkv_w_v), rwkv_w_o=cast(rwkv_w_o),
        rwkv_w0=row(rwkv_w0), rwkv_a0=row(rwkv_a0), rwkv_v0=row(rwkv_v0),
        rwkv_w12=_pad_lora(rwkv_w1, rwkv_w2), rwkv_a12=_pad_lora(rwkv_a1, rwkv_a2),
        rwkv_v12=_pad_lora(rwkv_v1, rwkv_v2), rwkv_g12=_pad_lora(rwkv_g1, rwkv_g2),
        rwkv_prm=jnp.stack([rwkv_k_k, rwkv_k_a, rwkv_r_k.reshape(n_rwkv, d), rwkv_ln_w, rwkv_ln_b,
                            zeros, zeros, zeros], axis=1),
        attn_w_qkv=cast(attn_w_qkv), attn_w_o=cast(attn_w_o), attn_rel_bias=attn_rel_bias,
    )


def kernel(x_prompt, x_sample, state_shift, state_wkv, cache_k, cache_v, p_prompt, p_sample,
           ln_g, ln_b, ffn_w_gate, ffn_w_up, ffn_w_down, ple_w_gate, ple_w_proj,
           rwkv_mu, rwkv_w_r, rwkv_w_k, rwkv_w_v, rwkv_w_o, rwkv_w0, rwkv_w1, rwkv_w2,
           rwkv_a0, rwkv_a1, rwkv_a2, rwkv_v0, rwkv_v1, rwkv_v2, rwkv_g1, rwkv_g2,
           rwkv_k_k, rwkv_k_a, rwkv_r_k, rwkv_ln_w, rwkv_ln_b,
           attn_w_qkv, attn_w_o, attn_rel_bias):
    w = _prepare_weights(ln_g, ln_b, ffn_w_gate, ffn_w_up, ffn_w_down, ple_w_gate, ple_w_proj,
                         rwkv_mu, rwkv_w_r, rwkv_w_k, rwkv_w_v, rwkv_w_o, rwkv_w0, rwkv_w1, rwkv_w2,
                         rwkv_a0, rwkv_a1, rwkv_a2, rwkv_v0, rwkv_v1, rwkv_v2, rwkv_g1, rwkv_g2,
                         rwkv_k_k, rwkv_k_a, rwkv_r_k, rwkv_ln_w, rwkv_ln_b,
                         attn_w_qkv, attn_w_o, attn_rel_bias)
    bp, _, d = x_prompt.shape
    n_rwkv = rwkv_mu.shape[0]
    heads = d // RWKV_HEAD
    zero_shift = jnp.zeros((bp, n_rwkv, d), x_prompt.dtype)
    zero_wkv = jnp.zeros((bp, n_rwkv, heads, RWKV_HEAD, RWKV_HEAD), x_prompt.dtype)
    y_p, shift_p, wkv_p, k_p, v_p = _trunk(x_prompt, p_prompt, zero_shift, zero_wkv, None, None, w, True)
    y_s, shift_s, wkv_s, k_s, v_s = _trunk(x_sample, p_sample, state_shift, state_wkv, cache_k, cache_v, w, False)
    return (y_p, y_s, shift_p, wkv_p, k_p, v_p, shift_s, wkv_s, k_s, v_s)
```

```python
import functools
import math

import jax
import jax.numpy as jnp
from jax import lax
from jax.experimental import pallas as pl
from jax.experimental.pallas import tpu as pltpu

F32 = jnp.float32
BF16 = jnp.bfloat16

CHUNK = 64
BAND_CHUNKS = 8
BAND_PAST = BAND_CHUNKS * CHUNK
REL_MAX = 256
ATTN_HEAD_DIM = 128
RWKV_HEAD = 64
LN_EPS = 1e-5
GN_EPS = 64e-5
NEG_INF = -1e30

LANES = 128
SLAB = 256
HEADS_PER_SLAB = SLAB // RWKV_HEAD
WKV_CHUNK = 64
WKV_SLABS_PER_STEP = 8
WKV_BLOCK_ELEMS = 512 * 1024
Q_BLK = 2 * CHUNK
KEY_WIN = BAND_PAST + Q_BLK
ATTN_GROUP = 8
VMEM_LIMIT = 56 * 1024 * 1024
MM_VMEM_BUDGET = 50 * 1024 * 1024


def _cparams(sem):
    return pltpu.CompilerParams(dimension_semantics=sem, vmem_limit_bytes=VMEM_LIMIT)


def _pow2_tile(m, cap):
    if m <= cap:
        return m
    t = 1 << (cap.bit_length() - 1)
    while m % t:
        t //= 2
    return t


def _lane_tile(n, cap):
    t = max(LANES, (cap // LANES) * LANES)
    while n % t:
        t -= LANES
    return t


def _dot(a, b):
    return jnp.dot(a.astype(BF16), b.astype(BF16), preferred_element_type=F32)


def _dot_nt(a, b):
    return lax.dot_general(a.astype(BF16), b.astype(BF16), (((1,), (1,)), ((), ())),
                           preferred_element_type=F32)


def _dot_tn(a, b):
    return lax.dot_general(a.astype(BF16), b.astype(BF16), (((0,), (0,)), ((), ())),
                           preferred_element_type=F32)


def _mm_kernel(x_ref, w_ref, o_ref):
    o_ref[...] = jnp.dot(x_ref[...], w_ref[...], preferred_element_type=F32).astype(o_ref.dtype)


def _mm_tiles(m, k, n, n_f32_blocks):
    tm = _pow2_tile(m, max(8, (12 << 20) // (2 * k)))
    tn = n
    while tn > LANES and (n % tn or 2 * (2 * tm * k + 2 * k * tn + 4 * n_f32_blocks * tm * tn) > MM_VMEM_BUDGET):
        tn -= LANES
    return tm, tn


def _wspec(w, lead, k, tn):
    assert w.ndim == len(lead) + 2
    return pl.BlockSpec((None,) * len(lead) + (k, tn), lambda i, j: tuple(lead) + (0, j))


def _mm_residual_kernel(x_ref, w_ref, r_ref, o_ref, *, alpha, scale):
    o_ref[...] = alpha * r_ref[...] + scale * jnp.dot(x_ref[...], w_ref[...], preferred_element_type=F32)


def _matmul(x, w, lead, out_dtype, name, residual=None):
    m, k = x.shape
    n = w.shape[-1]
    tm, tn = _mm_tiles(m, k, n, 1 if residual is None else 2)
    out_blk = pl.BlockSpec((tm, tn), lambda i, j: (i, j))
    in_specs = [pl.BlockSpec((tm, k), lambda i, j: (i, 0)), _wspec(w, lead, k, tn)]
    if residual is None:
        body, args = _mm_kernel, (x, w)
    else:
        r, alpha, scale = residual
        body, args = functools.partial(_mm_residual_kernel, alpha=alpha, scale=scale), (x, w, r)
        in_specs.append(out_blk)
    return pl.pallas_call(
        body,
        grid=(m // tm, n // tn),
        in_specs=in_specs,
        out_specs=out_blk,
        out_shape=jax.ShapeDtypeStruct((m, n), out_dtype),
        compiler_params=_cparams(("parallel", "arbitrary")),
        name=name,
    )(*args)


def _ffn_up_kernel(x_ref, wg_ref, wu_ref, o_ref):
    x = x_ref[...]
    g = jnp.dot(x, wg_ref[...], preferred_element_type=F32)
    u = jnp.dot(x, wu_ref[...], preferred_element_type=F32)
    o_ref[...] = (g * jax.nn.sigmoid(g) * u).astype(o_ref.dtype)


def _ffn_up(xb, wg, wu, lead):
    m, k = xb.shape
    n = wg.shape[-1]
    tm = _pow2_tile(m, max(8, (16 << 20) // (2 * k)))
    tn = _lane_tile(n, max(LANES, (2 << 20) // (2 * k)))
    return pl.pallas_call(
        _ffn_up_kernel,
        grid=(m // tm, n // tn),
        in_specs=[pl.BlockSpec((tm, k), lambda i, j: (i, 0)), _wspec(wg, lead, k, tn), _wspec(wu, lead, k, tn)],
        out_specs=pl.BlockSpec((tm, tn), lambda i, j: (i, j)),
        out_shape=jax.ShapeDtypeStruct((m, n), BF16),
        compiler_params=_cparams(("parallel", "arbitrary")),
        name="ffn_up",
    )(xb, wg, wu)


def _ln_kernel(z_ref, g_ref, b_ref, of_ref, *maybe_ob_ref):
    z = z_ref[...]
    mu = jnp.mean(z, axis=-1, keepdims=True)
    zc = z - mu
    var = jnp.mean(zc * zc, axis=-1, keepdims=True)
    o = zc * lax.rsqrt(var + LN_EPS) * g_ref[...] + b_ref[...]
    of_ref[...] = o
    for ob_ref in maybe_ob_ref:
        ob_ref[...] = o.astype(BF16)


def _layer_norm(z, g, b, with_bf16=True):
    m, d = z.shape
    tr = _pow2_tile(m, max(8, (4 << 20) // (4 * d)))
    row = pl.BlockSpec((tr, d), lambda i: (i, 0))
    vec = pl.BlockSpec((1, d), lambda i: (0, 0))
    n_out = 2 if with_bf16 else 1
    outs = pl.pallas_call(
        _ln_kernel,
        grid=(m // tr,),
        in_specs=[row, vec, vec],
        out_specs=[row] * n_out,
        out_shape=[jax.ShapeDtypeStruct((m, d), F32), jax.ShapeDtypeStruct((m, d), BF16)][:n_out],
        compiler_params=_cparams(("parallel",)),
        name="layer_norm",
    )(z, g.reshape(1, d), b.reshape(1, d))
    return (outs[0], outs[1]) if with_bf16 else (outs[0], None)


def _ple_kernel(xb_ref, wg_ref, p_ref, wp_ref, x_ref, of_ref, ob_ref):
    gate = jax.nn.sigmoid(jnp.dot(xb_ref[...], wg_ref[...], preferred_element_type=F32))
    emb = jnp.dot(p_ref[...].astype(BF16), wp_ref[...], preferred_element_type=F32)
    o = x_ref[...] + gate * emb
    of_ref[...] = o
    ob_ref[...] = o.astype(BF16)


def _ple(x, xb, p, wg, wp, lead):
    m, d = x.shape
    pd = p.shape[-1]
    tm = _pow2_tile(m, max(8, (8 << 20) // (2 * d)))
    tn = _lane_tile(d, max(LANES, (4 << 20) // (2 * d)))
    blk = pl.BlockSpec((tm, tn), lambda i, j: (i, j))
    return pl.pallas_call(
        _ple_kernel,
        grid=(m // tm, d // tn),
        in_specs=[pl.BlockSpec((tm, d), lambda i, j: (i, 0)), _wspec(wg, lead, d, tn),
                  pl.BlockSpec((None, tm, pd), lambda i, j: tuple(lead) + (i, 0)), _wspec(wp, lead, pd, tn), blk],
        out_specs=[blk, blk],
        out_shape=[jax.ShapeDtypeStruct((m, d), F32), jax.ShapeDtypeStruct((m, d), BF16)],
        compiler_params=_cparams(("parallel", "arbitrary")),
        name="ple",
    )(xb, wg, p, wp, x)


def _mix_kernel(x_ref, prev_ref, mu_ref, *o_refs):
    x = x_ref[...]
    shifted = pltpu.roll(x, 1, 0)
    first = lax.broadcasted_iota(jnp.int32, x.shape, 0) == 0
    shifted = jnp.where(first, prev_ref[...], shifted)
    xx = shifted - x
    for m, o_ref in enumerate(o_refs):
        o_ref[...] = (x + xx * mu_ref[m:m + 1, :]).astype(BF16)


def _token_mixes(x, x_prev, mu):
    b, t, d = x.shape
    tt = _pow2_tile(t, max(8, (4 << 20) // (4 * d)))
    nt = t // tt
    prev = jnp.concatenate([x_prev[:, None, :], x[:, tt - 1:t - 1:tt, :]], axis=1).reshape(b, nt, 1, d)
    blk = pl.BlockSpec((None, tt, d), lambda i, j: (i, j, 0))
    outs = pl.pallas_call(
        _mix_kernel,
        grid=(b, nt),
        in_specs=[blk,
                  pl.BlockSpec((None, None, 1, d), lambda i, j: (i, j, 0, 0)),
                  pl.BlockSpec((6, d), lambda i, j: (0, 0))],
        out_specs=[blk] * 6,
        out_shape=[jax.ShapeDtypeStruct((b, t, d), BF16)] * 6,
        compiler_params=_cparams(("parallel", "parallel")),
        name="token_mix",
    )(x, prev, mu)
    return [o.reshape(b * t, d) for o in outs]


def _lora_kernel(x_ref, w1_ref, w2_ref, *rest, mode):
    t = jnp.dot(x_ref[...], w1_ref[...], preferred_element_type=F32)
    if mode == "decay":
        t = jnp.tanh(t)
    elif mode == "gate":
        t = jax.nn.sigmoid(t)
    z = jnp.dot(t.astype(BF16), w2_ref[...], preferred_element_type=F32)
    if mode == "gate":
        (o_ref,) = rest
        o_ref[...] = z
    elif mode == "decay":
        bias_ref, o_ref = rest
        u = -(bias_ref[...] + z)
        softplus = jnp.maximum(u, 0.0) + jnp.log(1.0 + jnp.exp(-jnp.abs(u)))
        o_ref[...] = jnp.exp(-softplus - 0.5)
    elif mode == "sigmoid":
        bias_ref, o_ref = rest
        o_ref[...] = jax.nn.sigmoid(bias_ref[...] + z)
    else:
        bias_ref, v_ref, vf_ref, o_ref = rest
        v = v_ref[...]
        o_ref[...] = v + (vf_ref[...] - v) * jax.nn.sigmoid(bias_ref[...] + z)


def _lora(xb, w1, w2, bias, lead, mode, extra=()):
    m, d = xb.shape
    r = w1.shape[-1]
    tm = _pow2_tile(m, 256)
    row = pl.BlockSpec((tm, d), lambda i: (i, 0))
    whole = lambda a, b: pl.BlockSpec((None, a, b), lambda i: tuple(lead) + (0, 0))
    assert (bias is None) == (mode == "gate")
    biases = () if bias is None else (bias,)
    return pl.pallas_call(
        functools.partial(_lora_kernel, mode=mode),
        grid=(m // tm,),
        in_specs=[row, whole(d, r), whole(r, d)] + [whole(1, d)] * len(biases) + [row] * len(extra),
        out_specs=row,
        out_shape=jax.ShapeDtypeStruct((m, d), F32),
        compiler_params=_cparams(("parallel",)),
        name="lora_" + mode,
    )(xb, w1, w2, *biases, *extra)


def _pad_lora(w1, w2):
    r = w1.shape[-1]
    rp = -(-r // LANES) * LANES
    w1 = jnp.pad(w1.astype(BF16), ((0, 0), (0, 0), (0, rp - r)))
    w2 = jnp.pad(w2.astype(BF16), ((0, 0), (0, rp - r), (0, 0)))
    return w1, w2


def _round_robin(gens, batched=None):
    replies = [None] * len(gens)
    while gens:
        alive, asks = [], []
        for g, reply in zip(gens, replies):
            try:
                asks.append(g.send(reply))
                alive.append(g)
            except StopIteration:
                pass
        gens = alive
        replies = batched(asks) if asks and asks[0] is not None else [None] * len(gens)


def _split_bf16(x):
    hi = x.astype(BF16)
    lo = (x - hi.astype(F32)).astype(BF16)
    return hi, lo


def _wkv_kernel(r_ref, e_ref, k_ref, v_ref, a_ref, g_ref, prm_ref, s0_ref, o_ref, sf_ref, s_scr,
                *, chunk, n_chunks, n_slabs):
    c = chunk
    rows = HEADS_PER_SLAB * c
    tstep = pl.program_id(2)

    lane = lax.broadcasted_iota(jnp.int32, (1, SLAB), 1)
    head_masks = [(lane // RWKV_HEAD) == j for j in range(HEADS_PER_SLAB)]
    ri = lax.broadcasted_iota(jnp.int32, (rows, rows), 0)
    ci = lax.broadcasted_iota(jnp.int32, (rows, rows), 1)
    tri_strict = (ri % c) > (ci % c)
    tri_incl_cat = (lax.broadcasted_iota(jnp.int32, (c, rows), 0)
                    >= lax.broadcasted_iota(jnp.int32, (c, rows), 1) % c)
    eye = jnp.where(ri == ci, 1.0, 0.0).astype(F32)
    rs = lax.broadcasted_iota(jnp.int32, (SLAB, SLAB), 0)
    cs = lax.broadcasted_iota(jnp.int32, (SLAB, SLAB), 1)
    same_head = (rs // RWKV_HEAD) == (cs // RWKV_HEAD)
    ones_bd = jnp.where(same_head, 1.0, 0.0).astype(BF16)
    lt_incl = jnp.where(lax.broadcasted_iota(jnp.int32, (c, c), 0) >= lax.broadcasted_iota(jnp.int32, (c, c), 1),
                        1.0, 0.0).astype(BF16)

    @pl.when(tstep == 0)
    def _():
        for q in range(n_slabs):
            s0 = s0_ref[q]
            s_scr[q] = jnp.where(same_head, jnp.concatenate([s0] * HEADS_PER_SLAB, axis=0), 0.0)

    def stack(x):
        return jnp.concatenate([jnp.where(m, x, 0.0) for m in head_masks], axis=0)

    def fold(y):
        out = y[0:c]
        for j in range(1, HEADS_PER_SLAB):
            out = out + y[j * c:(j + 1) * c]
        return out

    def head_sums(asks):
        flat = [x.astype(BF16) for xs in asks for x in xs]
        tot = jnp.dot(jnp.concatenate(flat, axis=0), ones_bd, preferred_element_type=F32)
        parts = iter(tot[i * c:(i + 1) * c] for i in range(len(flat)))
        return [[next(parts) for _ in xs] for xs in asks]

    n_sq = int(math.log2(c)) - 1

    def chunk_step(q, sl):
        lanes = slice(q * SLAB, (q + 1) * SLAB)
        prm = prm_ref[:, lanes]
        k_k, k_a, r_k, ln_w, ln_b = (prm[i:i + 1, :] for i in range(5))
        r = r_ref[sl, lanes]
        e = e_ref[sl, lanes]
        k = k_ref[sl, lanes]
        v = v_ref[sl, lanes]
        a = a_ref[sl, lanes]

        kk = k * k_k
        k2 = k * (1.0 + (a - 1.0) * k_a)
        ss, bonus = yield [kk * kk, r * k2 * r_k]
        kkn = kk / jnp.maximum(jnp.sqrt(ss), 1e-12)
        av = -kkn
        bv = kkn * a

        e_hi, e_lo = _split_bf16(e)
        cum = (jnp.dot(lt_incl, e_hi, preferred_element_type=F32)
               + jnp.dot(lt_incl, e_lo, preferred_element_type=F32))
        yield
        cum_last = cum[c - 1:c, :]
        p_in = jnp.exp(-cum)
        p_ex = jnp.exp(e - cum)
        p_inv = jnp.exp(cum)
        p_end = jnp.exp(cum - cum_last)
        p_c = jnp.exp(-cum_last)

        b_st = stack(bv * p_inv)
        k_st = stack(k2 * p_inv)
        v_st = stack(v)
        ar = jnp.concatenate([stack(av * p_ex), r * p_in], axis=0).astype(BF16)
        xb = _dot_nt(ar, b_st)
        yield
        xk = _dot_nt(ar, k_st)
        yield
        a_ab = jnp.where(tri_strict, xb[:rows], 0.0)
        a_rb = jnp.where(tri_incl_cat, xb[rows:], 0.0)
        a_ak = jnp.where(tri_strict, xk[:rows], 0.0)
        a_rk = jnp.where(tri_incl_cat, xk[rows:], 0.0)

        x_pow = _dot(a_ab, a_ab)
        akv = _dot(a_ak, v_st)
        yield
        t_inv = eye + a_ab
        for _ in range(n_sq - 1):
            both = _dot(jnp.concatenate([t_inv, x_pow], axis=0), x_pow)
            yield
            t_inv = t_inv + both[:rows]
            x_pow = both[rows:]
        t_inv = t_inv + _dot(t_inv, x_pow)
        yield

        s_bd = s_scr[q]
        ars = _dot_nt(ar, s_bd)
        yield
        u_st = _dot(t_inv, ars[:rows] + akv)
        yield
        y = ars[rows:] + _dot(a_rb, u_st) + _dot(a_rk, v_st)
        yield
        uv = jnp.concatenate([fold(u_st), v], axis=0)
        bk = jnp.concatenate([bv * p_end, k2 * p_end], axis=0)
        s_scr[q] = s_bd * p_c + jnp.where(same_head, _dot_tn(uv, bk), 0.0)
        yield

        mean_hi, mean_lo = yield list(_split_bf16(y))
        yc = y - (mean_hi + mean_lo) * (1.0 / RWKV_HEAD)
        (var,) = yield [yc * yc]
        yn = yc * lax.rsqrt(var * (1.0 / RWKV_HEAD) + GN_EPS) * ln_w + ln_b
        o_ref[sl, lanes] = ((yn + bonus * v) * g_ref[sl, lanes]).astype(o_ref.dtype)

    def body(idx, carry):
        sl = pl.ds(pl.multiple_of(idx * c, c), c)
        _round_robin([chunk_step(q, sl) for q in range(n_slabs)], batched=head_sums)
        return carry

    lax.fori_loop(0, n_chunks, body, 0)

    @pl.when(tstep == pl.num_programs(2) - 1)
    def _():
        for q in range(n_slabs):
            s_bd = s_scr[q]
            out = s_bd[0:RWKV_HEAD]
            for j in range(1, HEADS_PER_SLAB):
                out = out + s_bd[j * RWKV_HEAD:(j + 1) * RWKV_HEAD]
            sf_ref[q] = out


def _wkv(r, e, k, v, a, g, prm, s0):
    b, t_in, d = r.shape
    n = RWKV_HEAD
    nslab = d // SLAB
    c = WKV_CHUNK
    t = -(-t_in // c) * c
    if t != t_in:
        r, e, k, v, a, g = (jnp.pad(z, ((0, 0), (0, t - t_in), (0, 0))) for z in (r, e, k, v, a, g))
    ns = max(s for s in (1, 2, 4, 8) if s <= WKV_SLABS_PER_STEP and nslab % s == 0)
    tb = _pow2_tile(t, max(c, WKV_BLOCK_ELEMS // (ns * SLAB)))
    s0_slab = s0.reshape(b, nslab, HEADS_PER_SLAB, n, n).transpose(0, 1, 3, 2, 4).reshape(b, nslab, n, SLAB)
    seq = pl.BlockSpec((None, tb, ns * SLAB), lambda i, s, j: (i, j, s))
    st = pl.BlockSpec((None, ns, n, SLAB), lambda i, s, j: (i, s, 0, 0))
    out, sf = pl.pallas_call(
        functools.partial(_wkv_kernel, chunk=c, n_chunks=tb // c, n_slabs=ns),
        grid=(b, nslab // ns, t // tb),
        in_specs=[seq] * 6 + [pl.BlockSpec((8, ns * SLAB), lambda i, s, j: (0, s)), st],
        out_specs=[seq, st],
        out_shape=[jax.ShapeDtypeStruct((b, t, d), BF16), jax.ShapeDtypeStruct((b, nslab, n, SLAB), F32)],
        scratch_shapes=[pltpu.VMEM((ns, SLAB, SLAB), F32)],
        compiler_params=_cparams(("parallel", "parallel", "arbitrary")),
        name="wkv7",
    )(r, e, k, v, a, g, prm, s0_slab)
    sf = sf.reshape(b, nslab, n, HEADS_PER_SLAB, n).transpose(0, 1, 3, 2, 4).reshape(b, d // n, n, n)
    return out[:, :t_in], sf


def _attn_prompt_kernel(q_ref, k_ref, v_ref, bias_ref, o_ref, kp, vp, *, t, scale):
    dh = kp.shape[1]
    zeros_past = jnp.zeros((BAND_PAST, dh), BF16)
    kp[0:BAND_PAST, :] = zeros_past
    vp[0:BAND_PAST, :] = zeros_past
    kp[BAND_PAST:, :] = k_ref[...].astype(BF16)
    vp[BAND_PAST:, :] = v_ref[...].astype(BF16)
    col = lax.broadcasted_iota(jnp.int32, (Q_BLK, KEY_WIN), 1)

    def block(c, mask_start):
        start = pl.multiple_of(c * Q_BLK, Q_BLK)
        s = _dot_nt(q_ref[pl.ds(start, Q_BLK), :], kp[pl.ds(start, KEY_WIN), :])
        yield
        s = s * scale + bias_ref[...]
        if mask_start:
            s = jnp.where(col >= BAND_PAST - c * Q_BLK, s, NEG_INF)
        m = jnp.max(s, axis=-1, keepdims=True)
        p = jnp.exp(s - m)
        p = p / jnp.sum(p, axis=-1, keepdims=True)
        o = _dot(p, vp[pl.ds(start, KEY_WIN), :])
        yield
        o_ref[pl.ds(start, Q_BLK), :] = o.astype(o_ref.dtype)

    def make_body(mask_start):
        def body(g, carry):
            _round_robin([block(g * ATTN_GROUP + i, mask_start) for i in range(ATTN_GROUP)])
            return carry
        return body

    n_grp = t // (Q_BLK * ATTN_GROUP)
    n_front = min(-(-BAND_PAST // (Q_BLK * ATTN_GROUP)), n_grp)
    lax.fori_loop(0, n_front, make_body(True), 0)
    lax.fori_loop(n_front, n_grp, make_body(False), 0)


def _attn_prompt(qkv, bias, b, t):
    d = qkv.shape[1] // 3
    h = d // ATTN_HEAD_DIM
    assert t % (ATTN_GROUP * Q_BLK) == 0
    col = lambda off: pl.BlockSpec((t, ATTN_HEAD_DIM), lambda i, j: (i, off + j))
    return pl.pallas_call(
        functools.partial(_attn_prompt_kernel, t=t, scale=ATTN_HEAD_DIM ** -0.5),
        grid=(b, h),
        in_specs=[col(0), col(h), col(2 * h),
                  pl.BlockSpec((None, Q_BLK, KEY_WIN), lambda i, j: (j, 0, 0))],
        out_specs=col(0),
        out_shape=jax.ShapeDtypeStruct((b * t, d), BF16),
        scratch_shapes=[pltpu.VMEM((BAND_PAST + t, ATTN_HEAD_DIM), BF16)] * 2,
        compiler_params=_cparams(("parallel", "parallel")),
        name="band_attn_prompt",
    )(qkv, qkv, qkv, bias)


def _attn_sample_kernel(q_ref, kn_ref, vn_ref, kc_ref, vc_ref, bias_ref, o_ref, kf, vf, *, rows, heads, scale):
    t = q_ref.shape[0]
    dh = ATTN_HEAD_DIM
    group = kf.shape[0]
    pad = jnp.zeros((kf.shape[1] - rows - t, dh), BF16)

    def head(h, slot):
        ln = slice(h * dh, (h + 1) * dh)
        for buf, cache_ref, new_ref in ((kf, kc_ref, kn_ref), (vf, vc_ref, vn_ref)):
            buf[slot, 0:rows, :] = cache_ref[pl.ds(h, rows, stride=heads), :].astype(BF16)
            buf[slot, rows:rows + t, :] = new_ref[:, ln].astype(BF16)
            buf[slot, rows + t:, :] = pad
        s = _dot_nt(q_ref[:, ln], kf[slot])
        yield
        s = s * scale + bias_ref[h]
        col = lax.broadcasted_iota(jnp.int32, s.shape, 1)
        s = jnp.where(col < rows + t, s, NEG_INF)
        m = jnp.max(s, axis=-1, keepdims=True)
        p = jnp.exp(s - m)
        p = p / jnp.sum(p, axis=-1, keepdims=True)
        o = _dot(p, vf[slot])
        yield
        o_ref[:, ln] = o.astype(o_ref.dtype)

    for h0 in range(0, heads, group):
        _round_robin([head(h0 + i, i) for i in range(group)])


def _attn_sample(qkv, cache_k, cache_v, layer, bias, b, t):
    d = qkv.shape[1] // 3
    h = d // ATTN_HEAD_DIM
    rows = cache_k.shape[2]
    win = bias.shape[2]
    group = max(s for s in (1, 2, 4) if h % s == 0)
    flat = lambda c: c.reshape(c.shape[0], c.shape[1], rows * h, ATTN_HEAD_DIM)
    col = lambda off: pl.BlockSpec((t, d), lambda i: (i, off))
    cache = pl.BlockSpec((None, None, rows * h, ATTN_HEAD_DIM), lambda i: (i, layer, 0, 0))
    return pl.pallas_call(
        functools.partial(_attn_sample_kernel, rows=rows, heads=h, scale=ATTN_HEAD_DIM ** -0.5),
        grid=(b,),
        in_specs=[col(0), col(1), col(2), cache, cache, pl.BlockSpec((h, t, win), lambda i: (0, 0, 0))],
        out_specs=col(0),
        out_shape=jax.ShapeDtypeStruct((b * t, d), BF16),
        scratch_shapes=[pltpu.VMEM((group, win, ATTN_HEAD_DIM), BF16)] * 2,
        compiler_params=_cparams(("parallel",)),
        name="band_attn_sample",
    )(qkv, qkv, qkv, flat(cache_k), flat(cache_v), bias)


def _rel_bias(table, n_q, n_k, offset):
    span = n_q + n_k - 1
    dist = (n_q - 1) - jnp.arange(span) + offset
    rev = table[:, jnp.clip(dist, -(CHUNK - 1), REL_MAX) + (CHUNK - 1)].astype(F32)
    period = jnp.pad(rev, ((0, 0), (0, 1)))
    shifted = jnp.tile(period, (1, n_q))[:, :n_q * span].reshape(table.shape[0], n_q, span)
    return shifted[:, :, n_q - 1:n_q - 1 + n_k]


def _trunk(x, p, shift0, wkv0, cache_k, cache_v, w, is_prompt):
    b, t, d = x.shape
    m = b * t
    depth = w["ln_g"].shape[0]
    alpha = (2 * depth) ** 0.25
    xf = x.reshape(m, d)
    xb = xf.astype(BF16)
    p = p.reshape(depth, m, p.shape[-1])
    v_first = None
    shifts, wkvs, ks, vs = [], [], [], []

    def ffn(xf, xb, i, s, ln_idx, with_bf16=True):
        hidden = _ffn_up(xb, w["ffn_gate"], w["ffn_up"], (i, s))
        z = _matmul(hidden, w["ffn_down"], (i, s), F32, "ffn_down", residual=(xf, alpha, 0.5))
        return _layer_norm(z, w["ln_g"][i, ln_idx], w["ln_b"][i, ln_idx], with_bf16)

    for i in range(depth):
        xf, xb = ffn(xf, xb, i, 0, 0, with_bf16=(i % 2 == 1))
        j = i // 2
        if i % 2 == 0:
            x3 = xf.reshape(b, t, d)
            xr, xw, xk, xv, xa, xg = _token_mixes(x3, shift0[:, j], w["rwkv_mu"][j])
            shifts.append(x3[:, -1])
            r = _matmul(xr, w["rwkv_w_r"], (j,), F32, "rwkv_r")
            k = _matmul(xk, w["rwkv_w_k"], (j,), F32, "rwkv_k")
            v = _matmul(xv, w["rwkv_w_v"], (j,), F32, "rwkv_v")
            e = _lora(xw, *w["rwkv_w12"], w["rwkv_w0"], (j,), "decay")
            a = _lora(xa, *w["rwkv_a12"], w["rwkv_a0"], (j,), "sigmoid")
            g = _lora(xg, *w["rwkv_g12"], None, (j,), "gate")
            if v_first is None:
                v_first = v
            else:
                v = _lora(xv, *w["rwkv_v12"], w["rwkv_v0"], (j - 1,), "vmix", extra=(v, v_first))
            to3 = lambda z: z.reshape(b, t, d)
            mix, s_fin = _wkv(to3(r), to3(e), to3(k), to3(v), to3(a), to3(g), w["rwkv_prm"][j], wkv0[:, j])
            wkvs.append(s_fin)
            z = _matmul(mix.reshape(m, d), w["rwkv_w_o"], (j,), F32, "rwkv_o", residual=(xf, alpha, 1.0))
        else:
            qkv = _matmul(xb, w["attn_w_qkv"], (j,), F32, "attn_qkv")
            kv3 = qkv.reshape(b, t, 3 * d)
            h = d // ATTN_HEAD_DIM
            if is_prompt:
                rows = min(BAND_PAST, t)
                qi = jnp.arange(Q_BLK)[:, None]
                kj = jnp.arange(KEY_WIN)[None, :]
                in_band = jnp.where(qi < CHUNK, kj < BAND_PAST + CHUNK, kj >= CHUNK)
                bias = jnp.where(in_band[None], _rel_bias(w["attn_rel_bias"][j], Q_BLK, KEY_WIN, BAND_PAST), NEG_INF)
                att = _attn_prompt(qkv, bias, b, t)
                ks.append(kv3[:, t - rows:, d:2 * d].reshape(b, rows, h, ATTN_HEAD_DIM))
                vs.append(kv3[:, t - rows:, 2 * d:].reshape(b, rows, h, ATTN_HEAD_DIM))
            else:
                rows = cache_k.shape[2]
                win = -(-(rows + t) // LANES) * LANES
                att = _attn_sample(qkv, cache_k, cache_v, j, _rel_bias(w["attn_rel_bias"][j], t, win, rows), b, t)
                ks.append(kv3[:, :, d:2 * d].reshape(b, t, h, ATTN_HEAD_DIM))
                vs.append(kv3[:, :, 2 * d:].reshape(b, t, h, ATTN_HEAD_DIM))
            z = _matmul(att, w["attn_w_o"], (j,), F32, "attn_o", residual=(xf, alpha, 1.0))
        xf, xb = _layer_norm(z, w["ln_g"][i, 1], w["ln_b"][i, 1])
        xf, xb = ffn(xf, xb, i, 1, 2)
        xf, xb = _ple(xf, xb, p, w["ple_w_gate"], w["ple_w_proj"], (i,))
    return (xf.reshape(b, t, d), jnp.stack(shifts, axis=1), jnp.stack(wkvs, axis=1),
            jnp.stack(ks, axis=1), jnp.stack(vs, axis=1))


def _prepare_weights(ln_g, ln_b, ffn_w_gate, ffn_w_up, ffn_w_down, ple_w_gate, ple_w_proj,
                     rwkv_mu, rwkv_w_r, rwkv_w_k, rwkv_w_v, rwkv_w_o, rwkv_w0, rwkv_w1, rwkv_w2,
                     rwkv_a0, rwkv_a1, rwkv_a2, rwkv_v0, rwkv_v1, rwkv_v2, rwkv_g1, rwkv_g2,
                     rwkv_k_k, rwkv_k_a, rwkv_r_k, rwkv_ln_w, rwkv_ln_b,
                     attn_w_qkv, attn_w_o, attn_rel_bias):
    n_rwkv = rwkv_mu.shape[0]
    d = ln_g.shape[-1]
    cast = lambda a: a.astype(BF16)
    row = lambda a: a.reshape(a.shape[0], 1, d)
    zeros = jnp.zeros((n_rwkv, d), F32)
    return dict(
        ln_g=ln_g, ln_b=ln_b,
        ffn_gate=cast(ffn_w_gate), ffn_up=cast(ffn_w_up), ffn_down=cast(ffn_w_down),
        ple_w_gate=cast(ple_w_gate), ple_w_proj=cast(ple_w_proj),
        rwkv_mu=rwkv_mu,
        rwkv_w_r=cast(rwkv_w_r), rwkv_w_k=cast(rwkv_w_k), rwkv_w_v=cast(rwkv_w_v), rwkv_w_o=cast(rwkv_w_o),
        rwkv_w0=row(rwkv_w0), rwkv_a0=row(rwkv_a0), rwkv_v0=row(rwkv_v0),
        rwkv_w12=_pad_lora(rwkv_w1, rwkv_w2), rwkv_a12=_pad_lora(rwkv_a1, rwkv_a2),
        rwkv_v12=_pad_lora(rwkv_v1, rwkv_v2), rwkv_g12=_pad_lora(rwkv_g1, rwkv_g2),
        rwkv_prm=jnp.stack([rwkv_k_k, rwkv_k_a, rwkv_r_k.reshape(n_rwkv, d), rwkv_ln_w, rwkv_ln_b,
                            zeros, zeros, zeros], axis=1),
        attn_w_qkv=cast(attn_w_qkv), attn_w_o=cast(attn_w_o), attn_rel_bias=attn_rel_bias,
    )


def kernel(x_prompt, x_sample, state_shift, state_wkv, cache_k, cache_v, p_prompt, p_sample,
           ln_g, ln_b, ffn_w_gate, ffn_w_up, ffn_w_down, ple_w_gate, ple_w_proj,
           rwkv_mu, rwkv_w_r, rwkv_w_k, rwkv_w_v, rwkv_w_o, rwkv_w0, rwkv_w1, rwkv_w2,
           rwkv_a0, rwkv_a1, rwkv_a2, rwkv_v0, rwkv_v1, rwkv_v2, rwkv_g1, rwkv_g2,
           rwkv_k_k, rwkv_k_a, rwkv_r_k, rwkv_ln_w, rwkv_ln_b,
           attn_w_qkv, attn_w_o, attn_rel_bias):
    w = _prepare_weights(ln_g, ln_b, ffn_w_gate, ffn_w_up, ffn_w_down, ple_w_gate, ple_w_proj,
                         rwkv_mu, rwkv_w_r, rwkv_w_k, rwkv_w_v, rwkv_w_o, rwkv_w0, rwkv_w1, rwkv_w2,
                         rwkv_a0, rwkv_a1, rwkv_a2, rwkv_v0, rwkv_v1, rwkv_v2, rwkv_g1, rwkv_g2,
                         rwkv_k_k, rwkv_k_a, rwkv_r_k, rwkv_ln_w, rwkv_ln_b,
                         attn_w_qkv, attn_w_o, attn_rel_bias)
    bp, _, d = x_prompt.shape
    n_rwkv = rwkv_mu.shape[0]
    heads = d // RWKV_HEAD
    zero_shift = jnp.zeros((bp, n_rwkv, d), x_prompt.dtype)
    zero_wkv = jnp.zeros((bp, n_rwkv, heads, RWKV_HEAD, RWKV_HEAD), x_prompt.dtype)
    y_p, shift_p, wkv_p, k_p, v_p = _trunk(x_prompt, p_prompt, zero_shift, zero_wkv, None, None, w, True)
    y_s, shift_s, wkv_s, k_s, v_s = _trunk(x_sample, p_sample, state_shift, state_wkv, cache_k, cache_v, w, False)
    return (y_p, y_s, shift_p, wkv_p, k_p, v_p, shift_s, wkv_s, k_s, v_s)
```

```python
import functools
import math

import jax
import jax.numpy as jnp
from jax import lax
from jax.experimental import pallas as pl
from jax.experimental.pallas import tpu as pltpu

F32 = jnp.float32
BF16 = jnp.bfloat16

CHUNK = 64
BAND_CHUNKS = 8
BAND_PAST = BAND_CHUNKS * CHUNK
REL_MAX = 256
ATTN_HEAD_DIM = 128
RWKV_HEAD = 64
LN_EPS = 1e-5
GN_EPS = 64e-5
NEG_INF = -1e30

LANES = 128
SLAB = 256
HEADS_PER_SLAB = SLAB // RWKV_HEAD
WKV_CHUNK = 64
WKV_SLABS_PER_STEP = 8
WKV_BLOCK_ELEMS = 512 * 1024
Q_BLK = 2 * CHUNK
KEY_WIN = BAND_PAST + Q_BLK
ATTN_GROUP = 8
VMEM_LIMIT = 56 * 1024 * 1024
MM_VMEM_BUDGET = 50 * 1024 * 1024


def _cparams(sem):
    return pltpu.CompilerParams(dimension_semantics=sem, vmem_limit_bytes=VMEM_LIMIT)


def _pow2_tile(m, cap):
    if m <= cap:
        return m
    t = 1 << (cap.bit_length() - 1)
    while m % t:
        t //= 2
    return t


def _lane_tile(n, cap):
    t = max(LANES, (cap // LANES) * LANES)
    while n % t:
        t -= LANES
    return t


def _dot(a, b):
    return jnp.dot(a.astype(BF16), b.astype(BF16), preferred_element_type=F32)


def _dot_nt(a, b):
    return lax.dot_general(a.astype(BF16), b.astype(BF16), (((1,), (1,)), ((), ())),
                           preferred_element_type=F32)


def _dot_tn(a, b):
    return lax.dot_general(a.astype(BF16), b.astype(BF16), (((0,), (0,)), ((), ())),
                           preferred_element_type=F32)


def _mm_kernel(x_ref, w_ref, o_ref):
    o_ref[...] = jnp.dot(x_ref[...], w_ref[...], preferred_element_type=F32).astype(o_ref.dtype)


def _mm_tiles(m, k, n, n_f32_blocks):
    tm = _pow2_tile(m, max(8, (12 << 20) // (2 * k)))
    tn = n
    while tn > LANES and (n % tn or 2 * (2 * tm * k + 2 * k * tn + 4 * n_f32_blocks * tm * tn) > MM_VMEM_BUDGET):
        tn -= LANES
    return tm, tn


def _wspec(w, lead, k, tn):
    assert w.ndim == len(lead) + 2
    return pl.BlockSpec((None,) * len(lead) + (k, tn), lambda i, j: tuple(lead) + (0, j))


def _mm_residual_kernel(x_ref, w_ref, r_ref, o_ref, *, alpha, scale):
    o_ref[...] = alpha * r_ref[...] + scale * jnp.dot(x_ref[...], w_ref[...], preferred_element_type=F32)


def _matmul(x, w, lead, out_dtype, name, residual=None):
    m, k = x.shape
    n = w.shape[-1]
    tm, tn = _mm_tiles(m, k, n, 1 if residual is None else 2)
    out_blk = pl.BlockSpec((tm, tn), lambda i, j: (i, j))
    in_specs = [pl.BlockSpec((tm, k), lambda i, j: (i, 0)), _wspec(w, lead, k, tn)]
    if residual is None:
        body, args = _mm_kernel, (x, w)
    else:
        r, alpha, scale = residual
        body, args = functools.partial(_mm_residual_kernel, alpha=alpha, scale=scale), (x, w, r)
        in_specs.append(out_blk)
    return pl.pallas_call(
        body,
        grid=(m // tm, n // tn),
        in_specs=in_specs,
        out_specs=out_blk,
        out_shape=jax.ShapeDtypeStruct((m, n), out_dtype),
        compiler_params=_cparams(("parallel", "arbitrary")),
        name=name,
    )(*args)


def _ffn_up_kernel(x_ref, wg_ref, wu_ref, o_ref):
    x = x_ref[...]
    g = jnp.dot(x, wg_ref[...], preferred_element_type=F32)
    u = jnp.dot(x, wu_ref[...], preferred_element_type=F32)
    o_ref[...] = (g * jax.nn.sigmoid(g) * u).astype(o_ref.dtype)


def _ffn_up(xb, wg, wu, lead):
    m, k = xb.shape
    n = wg.shape[-1]
    tm = _pow2_tile(m, max(8, (16 << 20) // (2 * k)))
    tn = _lane_tile(n, max(LANES, (2 << 20) // (2 * k)))
    return pl.pallas_call(
        _ffn_up_kernel,
        grid=(m // tm, n // tn),
        in_specs=[pl.BlockSpec((tm, k), lambda i, j: (i, 0)), _wspec(wg, lead, k, tn), _wspec(wu, lead, k, tn)],
        out_specs=pl.BlockSpec((tm, tn), lambda i, j: (i, j)),
        out_shape=jax.ShapeDtypeStruct((m, n), BF16),
        compiler_params=_cparams(("parallel", "arbitrary")),
        name="ffn_up",
    )(xb, wg, wu)


def _ln_kernel(z_ref, g_ref, b_ref, of_ref, *maybe_ob_ref):
    z = z_ref[...]
    mu = jnp.mean(z, axis=-1, keepdims=True)
    zc = z - mu
    var = jnp.mean(zc * zc, axis=-1, keepdims=True)
    o = zc * lax.rsqrt(var + LN_EPS) * g_ref[...] + b_ref[...]
    of_ref[...] = o
    for ob_ref in maybe_ob_ref:
        ob_ref[...] = o.astype(BF16)


def _layer_norm(z, g, b, with_bf16=True):
    m, d = z.shape
    tr = _pow2_tile(m, max(8, (4 << 20) // (4 * d)))
    row = pl.BlockSpec((tr, d), lambda i: (i, 0))
    vec = pl.BlockSpec((1, d), lambda i: (0, 0))
    n_out = 2 if with_bf16 else 1
    outs = pl.pallas_call(
        _ln_kernel,
        grid=(m // tr,),
        in_specs=[row, vec, vec],
        out_specs=[row] * n_out,
        out_shape=[jax.ShapeDtypeStruct((m, d), F32), jax.ShapeDtypeStruct((m, d), BF16)][:n_out],
        compiler_params=_cparams(("parallel",)),
        name="layer_norm",
    )(z, g.reshape(1, d), b.reshape(1, d))
    return (outs[0], outs[1]) if with_bf16 else (outs[0], None)


def _ple_kernel(xb_ref, wg_ref, p_ref, wp_ref, x_ref, of_ref, ob_ref):
    gate = jax.nn.sigmoid(jnp.dot(xb_ref[...], wg_ref[...], preferred_element_type=F32))
    emb = jnp.dot(p_ref[...].astype(BF16), wp_ref[...], preferred_element_type=F32)
    o = x_ref[...] + gate * emb
    of_ref[...] = o
    ob_ref[...] = o.astype(BF16)


def _ple(x, xb, p, wg, wp, lead):
    m, d = x.shape
    pd = p.shape[-1]
    tm = _pow2_tile(m, max(8, (8 << 20) // (2 * d)))
    tn = _lane_tile(d, max(LANES, (4 << 20) // (2 * d)))
    blk = pl.BlockSpec((tm, tn), lambda i, j: (i, j))
    return pl.pallas_call(
        _ple_kernel,
        grid=(m // tm, d // tn),
        in_specs=[pl.BlockSpec((tm, d), lambda i, j: (i, 0)), _wspec(wg, lead, d, tn),
                  pl.BlockSpec((None, tm, pd), lambda i, j: tuple(lead) + (i, 0)), _wspec(wp, lead, pd, tn), blk],
        out_specs=[blk, blk],
        out_shape=[jax.ShapeDtypeStruct((m, d), F32), jax.ShapeDtypeStruct((m, d), BF16)],
        compiler_params=_cparams(("parallel", "arbitrary")),
        name="ple",
    )(xb, wg, p, wp, x)


def _mix_kernel(x_ref, prev_ref, mu_ref, w1_ref, a1_ref, g1_ref, *rest, with_v):
    if with_v:
        v1_ref, xr_ref, xk_ref, xv_ref, tw_ref, ta_ref, tg_ref, tv_ref = rest
    else:
        xr_ref, xk_ref, xv_ref, tw_ref, ta_ref, tg_ref = rest
    x = x_ref[...]
    shifted = pltpu.roll(x, 1, 0)
    first = lax.broadcasted_iota(jnp.int32, x.shape, 0) == 0
    shifted = jnp.where(first, prev_ref[...], shifted)
    xx = shifted - x
    mix = lambda m: (x + xx * mu_ref[m:m + 1, :]).astype(BF16)
    down = lambda xm, w_ref: jnp.dot(xm, w_ref[...], preferred_element_type=F32)
    xr_ref[...] = mix(0)
    tw_ref[...] = jnp.tanh(down(mix(1), w1_ref)).astype(BF16)
    xk_ref[...] = mix(2)
    xv = mix(3)
    xv_ref[...] = xv
    ta_ref[...] = down(mix(4), a1_ref).astype(BF16)
    tg_ref[...] = jax.nn.sigmoid(down(mix(5), g1_ref)).astype(BF16)
    if with_v:
        tv_ref[...] = down(xv, v1_ref).astype(BF16)


def _token_mixes(x, x_prev, mu, w1, a1, g1, v1, layer):
    b, t, d = x.shape
    tt = _pow2_tile(t, max(8, (4 << 20) // (4 * d)))
    nt = t // tt
    prev = jnp.concatenate([x_prev[:, None, :], x[:, tt - 1:t - 1:tt, :]], axis=1).reshape(b, nt, 1, d)
    blk = pl.BlockSpec((None, tt, d), lambda i, j: (i, j, 0))
    wspec = lambda w, idx: pl.BlockSpec((None, d, w.shape[-1]), lambda i, j: (idx, 0, 0))
    tspec = lambda w: pl.BlockSpec((None, tt, w.shape[-1]), lambda i, j: (i, j, 0))
    tshape = lambda w: jax.ShapeDtypeStruct((b, t, w.shape[-1]), BF16)
    lows = [(w1, layer), (a1, layer), (g1, layer)] + ([(v1, layer - 1)] if layer > 0 else [])
    outs = pl.pallas_call(
        functools.partial(_mix_kernel, with_v=layer > 0),
        grid=(b, nt),
        in_specs=[blk,
                  pl.BlockSpec((None, None, 1, d), lambda i, j: (i, j, 0, 0)),
                  pl.BlockSpec((6, d), lambda i, j: (0, 0))] + [wspec(w, idx) for w, idx in lows],
        out_specs=[blk] * 3 + [tspec(w) for w, _ in lows],
        out_shape=[jax.ShapeDtypeStruct((b, t, d), BF16)] * 3 + [tshape(w) for w, _ in lows],
        compiler_params=_cparams(("parallel", "parallel")),
        name="token_mix",
    )(x, prev, mu, *[w for w, _ in lows])
    return [o.reshape(b * t, d) for o in outs[:3]], list(outs[3:])


def _pad_lora(w1, w2):
    r = w1.shape[-1]
    rp = -(-r // LANES) * LANES
    w1 = jnp.pad(w1.astype(BF16), ((0, 0), (0, 0), (0, rp - r)))
    w2 = jnp.pad(w2.astype(BF16), ((0, 0), (0, rp - r), (0, 0)))
    return w1, w2


def _round_robin(gens, batched=None):
    replies = [None] * len(gens)
    while gens:
        alive, asks = [], []
        for g, reply in zip(gens, replies):
            try:
                asks.append(g.send(reply))
                alive.append(g)
            except StopIteration:
                pass
        gens = alive
        replies = batched(asks) if asks and asks[0] is not None else [None] * len(gens)


def _split_bf16(x):
    hi = x.astype(BF16)
    lo = (x - hi.astype(F32)).astype(BF16)
    return hi, lo


def _wkv_kernel(*refs, chunk, n_chunks, n_slabs, with_v, t_valid):
    if with_v:
        (r_ref, k_ref, v_ref, vf_ref, tw_ref, ta_ref, tg_ref, tv_ref, w2_ref, a2_ref, g2_ref, v2_ref,
         prm_ref, s0_ref, o_ref, sf_ref, s_scr) = refs
    else:
        (r_ref, k_ref, v_ref, tw_ref, ta_ref, tg_ref, w2_ref, a2_ref, g2_ref,
         prm_ref, s0_ref, o_ref, sf_ref, s_scr) = refs
    c = chunk
    rows = HEADS_PER_SLAB * c
    tstep = pl.program_id(2)
    t_block = n_chunks * c

    lane = lax.broadcasted_iota(jnp.int32, (1, SLAB), 1)
    head_masks = [(lane // RWKV_HEAD) == j for j in range(HEADS_PER_SLAB)]
    ri = lax.broadcasted_iota(jnp.int32, (rows, rows), 0)
    ci = lax.broadcasted_iota(jnp.int32, (rows, rows), 1)
    tri_strict = (ri % c) > (ci % c)
    tri_incl_cat = (lax.broadcasted_iota(jnp.int32, (c, rows), 0)
                    >= lax.broadcasted_iota(jnp.int32, (c, rows), 1) % c)
    eye = jnp.where(ri == ci, 1.0, 0.0).astype(F32)
    rs = lax.broadcasted_iota(jnp.int32, (SLAB, SLAB), 0)
    cs = lax.broadcasted_iota(jnp.int32, (SLAB, SLAB), 1)
    same_head = (rs // RWKV_HEAD) == (cs // RWKV_HEAD)
    ones_bd = jnp.where(same_head, 1.0, 0.0).astype(BF16)
    lt_incl = jnp.where(lax.broadcasted_iota(jnp.int32, (c, c), 0) >= lax.broadcasted_iota(jnp.int32, (c, c), 1),
                        1.0, 0.0).astype(BF16)

    @pl.when(tstep == 0)
    def _():
        for q in range(n_slabs):
            s0 = s0_ref[q]
            s_scr[q] = jnp.where(same_head, jnp.concatenate([s0] * HEADS_PER_SLAB, axis=0), 0.0)

    def stack(x):
        return jnp.concatenate([jnp.where(m, x, 0.0) for m in head_masks], axis=0)

    def fold(y):
        out = y[0:c]
        for j in range(1, HEADS_PER_SLAB):
            out = out + y[j * c:(j + 1) * c]
        return out

    def head_sums(asks):
        flat = [x.astype(BF16) for xs in asks for x in xs]
        tot = jnp.dot(jnp.concatenate(flat, axis=0), ones_bd, preferred_element_type=F32)
        parts = iter(tot[i * c:(i + 1) * c] for i in range(len(flat)))
        return [[next(parts) for _ in xs] for xs in asks]

    n_sq = int(math.log2(c)) - 1

    def chunk_step(q, start):
        sl = pl.ds(start, c)
        lanes = slice(q * SLAB, (q + 1) * SLAB)
        prm = prm_ref[:, lanes]
        k_k, k_a, r_k, ln_w, ln_b, w0, a0, v0 = (prm[i:i + 1, :] for i in range(8))
        up = lambda t_ref, w_ref: jnp.dot(t_ref[sl, :], w_ref[:, lanes], preferred_element_type=F32)
        r = r_ref[sl, lanes]
        k = k_ref[sl, lanes]
        v = v_ref[sl, lanes]
        u = -(w0 + up(tw_ref, w2_ref))
        e = jnp.exp(-(jnp.maximum(u, 0.0) + jnp.log(1.0 + jnp.exp(-jnp.abs(u)))) - 0.5)
        if t_valid is not None:
            frame = tstep * t_block + start + lax.broadcasted_iota(jnp.int32, (c, SLAB), 0)
            e = jnp.where(frame < t_valid, e, 0.0)
        a = jax.nn.sigmoid(a0 + up(ta_ref, a2_ref))
        g = up(tg_ref, g2_ref)
        if with_v:
            v = v + (vf_ref[sl, lanes] - v) * jax.nn.sigmoid(v0 + up(tv_ref, v2_ref))
        yield

        kk = k * k_k
        k2 = k * (1.0 + (a - 1.0) * k_a)
        ss, bonus = yield [kk * kk, r * k2 * r_k]
        kkn = kk / jnp.maximum(jnp.sqrt(ss), 1e-12)
        av = -kkn
        bv = kkn * a

        e_hi, e_lo = _split_bf16(e)
        cum = (jnp.dot(lt_incl, e_hi, preferred_element_type=F32)
               + jnp.dot(lt_incl, e_lo, preferred_element_type=F32))
        yield
        cum_last = cum[c - 1:c, :]
        p_in = jnp.exp(-cum)
        p_ex = jnp.exp(e - cum)
        p_inv = jnp.exp(cum)
        p_end = jnp.exp(cum - cum_last)
        p_c = jnp.exp(-cum_last)

        b_st = stack(bv * p_inv)
        k_st = stack(k2 * p_inv)
        v_st = stack(v)
        ar = jnp.concatenate([stack(av * p_ex), r * p_in], axis=0).astype(BF16)
        xb = _dot_nt(ar, b_st)
        yield
        xk = _dot_nt(ar, k_st)
        yield
        a_ab = jnp.where(tri_strict, xb[:rows], 0.0)
        a_rb = jnp.where(tri_incl_cat, xb[rows:], 0.0)
        a_ak = jnp.where(tri_strict, xk[:rows], 0.0)
        a_rk = jnp.where(tri_incl_cat, xk[rows:], 0.0)

        x_pow = _dot(a_ab, a_ab)
        akv = _dot(a_ak, v_st)
        yield
        t_inv = eye + a_ab
        for _ in range(n_sq - 1):
            both = _dot(jnp.concatenate([t_inv, x_pow], axis=0), x_pow)
            yield
            t_inv = t_inv + both[:rows]
            x_pow = both[rows:]
        t_inv = t_inv + _dot(t_inv, x_pow)
        yield

        s_bd = s_scr[q]
        ars = _dot_nt(ar, s_bd)
        yield
        u_st = _dot(t_inv, ars[:rows] + akv)
        yield
        y = ars[rows:] + _dot(a_rb, u_st) + _dot(a_rk, v_st)
        yield
        uv = jnp.concatenate([fold(u_st), v], axis=0)
        bk = jnp.concatenate([bv * p_end, k2 * p_end], axis=0)
        s_scr[q] = s_bd * p_c + jnp.where(same_head, _dot_tn(uv, bk), 0.0)
        yield

        mean_hi, mean_lo = yield list(_split_bf16(y))
        yc = y - (mean_hi + mean_lo) * (1.0 / RWKV_HEAD)
        (var,) = yield [yc * yc]
        yn = yc * lax.rsqrt(var * (1.0 / RWKV_HEAD) + GN_EPS) * ln_w + ln_b
        o_ref[sl, lanes] = ((yn + bonus * v) * g).astype(o_ref.dtype)

    def body(idx, carry):
        start = pl.multiple_of(idx * c, c)
        _round_robin([chunk_step(q, start) for q in range(n_slabs)], batched=head_sums)
        return carry

    lax.fori_loop(0, n_chunks, body, 0)

    @pl.when(tstep == pl.num_programs(2) - 1)
    def _():
        for q in range(n_slabs):
            s_bd = s_scr[q]
            out = s_bd[0:RWKV_HEAD]
            for j in range(1, HEADS_PER_SLAB):
                out = out + s_bd[j * RWKV_HEAD:(j + 1) * RWKV_HEAD]
            sf_ref[q] = out


def _wkv(r, k, v, v_first, lows, ups, prm, layer, s0):
    b, t_in, d = r.shape
    n = RWKV_HEAD
    nslab = d // SLAB
    c = WKV_CHUNK
    t = -(-t_in // c) * c
    with_v = v_first is not None
    seqs = [r, k, v] + ([v_first] if with_v else [])
    if t != t_in:
        pad = lambda z: jnp.pad(z, ((0, 0), (0, t - t_in), (0, 0)))
        seqs, lows = [pad(z) for z in seqs], [pad(z) for z in lows]
    ns = max(s for s in (1, 2, 4, 8) if s <= WKV_SLABS_PER_STEP and nslab % s == 0)
    tb = _pow2_tile(t, max(c, WKV_BLOCK_ELEMS // (ns * SLAB)))
    s0_slab = s0.reshape(b, nslab, HEADS_PER_SLAB, n, n).transpose(0, 1, 3, 2, 4).reshape(b, nslab, n, SLAB)
    seq = pl.BlockSpec((None, tb, ns * SLAB), lambda i, s, j: (i, j, s))
    low = lambda z: pl.BlockSpec((None, tb, z.shape[-1]), lambda i, s, j: (i, j, 0))
    up = lambda w, idx: pl.BlockSpec((None, w.shape[1], ns * SLAB), lambda i, s, j: (idx, 0, s))
    st = pl.BlockSpec((None, ns, n, SLAB), lambda i, s, j: (i, s, 0, 0))
    out, sf = pl.pallas_call(
        functools.partial(_wkv_kernel, chunk=c, n_chunks=tb // c, n_slabs=ns, with_v=with_v,
                          t_valid=t_in if t != t_in else None),
        grid=(b, nslab // ns, t // tb),
        in_specs=([seq] * len(seqs) + [low(z) for z in lows] + [up(w, idx) for w, idx in ups]
                  + [pl.BlockSpec((None, 8, ns * SLAB), lambda i, s, j: (layer, 0, s)), st]),
        out_specs=[seq, st],
        out_shape=[jax.ShapeDtypeStruct((b, t, d), BF16), jax.ShapeDtypeStruct((b, nslab, n, SLAB), F32)],
        scratch_shapes=[pltpu.VMEM((ns, SLAB, SLAB), F32)],
        compiler_params=_cparams(("parallel", "parallel", "arbitrary")),
        name="wkv7",
    )(*seqs, *lows, *[w for w, _ in ups], prm, s0_slab)
    sf = sf.reshape(b, nslab, n, HEADS_PER_SLAB, n).transpose(0, 1, 3, 2, 4).reshape(b, d // n, n, n)
    return out[:, :t_in], sf


def _attn_prompt_kernel(q_ref, k_ref, v_ref, bias_ref, o_ref, kp, vp, *, t, scale):
    dh = kp.shape[1]
    zeros_past = jnp.zeros((BAND_PAST, dh), BF16)
    kp[0:BAND_PAST, :] = zeros_past
    vp[0:BAND_PAST, :] = zeros_past
    kp[BAND_PAST:, :] = k_ref[...].astype(BF16)
    vp[BAND_PAST:, :] = v_ref[...].astype(BF16)
    col = lax.broadcasted_iota(jnp.int32, (Q_BLK, KEY_WIN), 1)

    def block(c, mask_start):
        start = pl.multiple_of(c * Q_BLK, Q_BLK)
        s = _dot_nt(q_ref[pl.ds(start, Q_BLK), :], kp[pl.ds(start, KEY_WIN), :])
        yield
        s = s * scale + bias_ref[...]
        if mask_start:
            s = jnp.where(col >= BAND_PAST - c * Q_BLK, s, NEG_INF)
        m = jnp.max(s, axis=-1, keepdims=True)
        p = jnp.exp(s - m)
        p = p / jnp.sum(p, axis=-1, keepdims=True)
        o = _dot(p, vp[pl.ds(start, KEY_WIN), :])
        yield
        o_ref[pl.ds(start, Q_BLK), :] = o.astype(o_ref.dtype)

    def make_body(mask_start):
        def body(g, carry):
            _round_robin([block(g * ATTN_GROUP + i, mask_start) for i in range(ATTN_GROUP)])
            return carry
        return body

    n_grp = t // (Q_BLK * ATTN_GROUP)
    n_front = min(-(-BAND_PAST // (Q_BLK * ATTN_GROUP)), n_grp)
    lax.fori_loop(0, n_front, make_body(True), 0)
    lax.fori_loop(n_front, n_grp, make_body(False), 0)


def _attn_prompt(qkv, bias, b, t):
    d = qkv.shape[1] // 3
    h = d // ATTN_HEAD_DIM
    assert t % (ATTN_GROUP * Q_BLK) == 0
    col = lambda off: pl.BlockSpec((t, ATTN_HEAD_DIM), lambda i, j: (i, off + j))
    return pl.pallas_call(
        functools.partial(_attn_prompt_kernel, t=t, scale=ATTN_HEAD_DIM ** -0.5),
        grid=(b, h),
        in_specs=[col(0), col(h), col(2 * h),
                  pl.BlockSpec((None, Q_BLK, KEY_WIN), lambda i, j: (j, 0, 0))],
        out_specs=col(0),
        out_shape=jax.ShapeDtypeStruct((b * t, d), BF16),
        scratch_shapes=[pltpu.VMEM((BAND_PAST + t, ATTN_HEAD_DIM), BF16)] * 2,
        compiler_params=_cparams(("parallel", "parallel")),
        name="band_attn_prompt",
    )(qkv, qkv, qkv, bias)


def _attn_sample_kernel(q_ref, kn_ref, vn_ref, kc_ref, vc_ref, bias_ref, o_ref, kf, vf, *, rows, heads, scale):
    t = q_ref.shape[0]
    dh = ATTN_HEAD_DIM
    group = kf.shape[0]
    pad = jnp.zeros((kf.shape[1] - rows - t, dh), BF16)

    def head(h, slot):
        ln = slice(h * dh, (h + 1) * dh)
        for buf, cache_ref, new_ref in ((kf, kc_ref, kn_ref), (vf, vc_ref, vn_ref)):
            buf[slot, 0:rows, :] = cache_ref[pl.ds(h, rows, stride=heads), :].astype(BF16)
            buf[slot, rows:rows + t, :] = new_ref[:, ln].astype(BF16)
            buf[slot, rows + t:, :] = pad
        s = _dot_nt(q_ref[:, ln], kf[slot])
        yield
        s = s * scale + bias_ref[h]
        col = lax.broadcasted_iota(jnp.int32, s.shape, 1)
        s = jnp.where(col < rows + t, s, NEG_INF)
        m = jnp.max(s, axis=-1, keepdims=True)
        p = jnp.exp(s - m)
        p = p / jnp.sum(p, axis=-1, keepdims=True)
        o = _dot(p, vf[slot])
        yield
        o_ref[:, ln] = o.astype(o_ref.dtype)

    for h0 in range(0, heads, group):
        _round_robin([head(h0 + i, i) for i in range(group)])


def _attn_sample(qkv, cache_k, cache_v, layer, bias, b, t):
    d = qkv.shape[1] // 3
    h = d // ATTN_HEAD_DIM
    rows = cache_k.shape[2]
    win = bias.shape[2]
    group = max(s for s in (1, 2, 4) if h % s == 0)
    flat = lambda c: c.reshape(c.shape[0], c.shape[1], rows * h, ATTN_HEAD_DIM)
    col = lambda off: pl.BlockSpec((t, d), lambda i: (i, off))
    cache = pl.BlockSpec((None, None, rows * h, ATTN_HEAD_DIM), lambda i: (i, layer, 0, 0))
    return pl.pallas_call(
        functools.partial(_attn_sample_kernel, rows=rows, heads=h, scale=ATTN_HEAD_DIM ** -0.5),
        grid=(b,),
        in_specs=[col(0), col(1), col(2), cache, cache, pl.BlockSpec((h, t, win), lambda i: (0, 0, 0))],
        out_specs=col(0),
        out_shape=jax.ShapeDtypeStruct((b * t, d), BF16),
        scratch_shapes=[pltpu.VMEM((group, win, ATTN_HEAD_DIM), BF16)] * 2,
        compiler_params=_cparams(("parallel",)),
        name="band_attn_sample",
    )(qkv, qkv, qkv, flat(cache_k), flat(cache_v), bias)


def _rel_bias(table, n_q, n_k, offset):
    span = n_q + n_k - 1
    dist = (n_q - 1) - jnp.arange(span) + offset
    rev = table[:, jnp.clip(dist, -(CHUNK - 1), REL_MAX) + (CHUNK - 1)].astype(F32)
    period = jnp.pad(rev, ((0, 0), (0, 1)))
    shifted = jnp.tile(period, (1, n_q))[:, :n_q * span].reshape(table.shape[0], n_q, span)
    return shifted[:, :, n_q - 1:n_q - 1 + n_k]


def _trunk(x, p, shift0, wkv0, cache_k, cache_v, w, is_prompt):
    b, t, d = x.shape
    m = b * t
    depth = w["ln_g"].shape[0]
    alpha = (2 * depth) ** 0.25
    xf = x.reshape(m, d)
    xb = xf.astype(BF16)
    p = p.reshape(depth, m, p.shape[-1])
    v_first = None
    shifts, wkvs, ks, vs = [], [], [], []

    def ffn(xf, xb, i, s, ln_idx, with_bf16=True):
        hidden = _ffn_up(xb, w["ffn_gate"], w["ffn_up"], (i, s))
        z = _matmul(hidden, w["ffn_down"], (i, s), F32, "ffn_down", residual=(xf, alpha, 0.5))
        return _layer_norm(z, w["ln_g"][i, ln_idx], w["ln_b"][i, ln_idx], with_bf16)

    for i in range(depth):
        xf, xb = ffn(xf, xb, i, 0, 0, with_bf16=(i % 2 == 1))
        j = i // 2
        if i % 2 == 0:
            x3 = xf.reshape(b, t, d)
            (w1, w2), (a1, a2), (g1, g2), (v1, v2) = (w[n] for n in ("rwkv_w12", "rwkv_a12", "rwkv_g12", "rwkv_v12"))
            (xr, xk, xv), lows = _token_mixes(x3, shift0[:, j], w["rwkv_mu"][j], w1, a1, g1, v1, j)
            shifts.append(x3[:, -1])
            to3 = lambda z: z.reshape(b, t, d)
            r = to3(_matmul(xr, w["rwkv_w_r"], (j,), F32, "rwkv_r"))
            k = to3(_matmul(xk, w["rwkv_w_k"], (j,), F32, "rwkv_k"))
            v = to3(_matmul(xv, w["rwkv_w_v"], (j,), F32, "rwkv_v"))
            ups = [(w2, j), (a2, j), (g2, j)] + ([(v2, j - 1)] if j > 0 else [])
            mix, s_fin = _wkv(r, k, v, v_first if j > 0 else None, lows, ups, w["rwkv_prm"], j, wkv0[:, j])
            if j == 0:
                v_first = v
            wkvs.append(s_fin)
            z = _matmul(mix.reshape(m, d), w["rwkv_w_o"], (j,), F32, "rwkv_o", residual=(xf, alpha, 1.0))
        else:
            qkv = _matmul(xb, w["attn_w_qkv"], (j,), F32, "attn_qkv")
            kv3 = qkv.reshape(b, t, 3 * d)
            h = d // ATTN_HEAD_DIM
            if is_prompt:
                rows = min(BAND_PAST, t)
                qi = jnp.arange(Q_BLK)[:, None]
                kj = jnp.arange(KEY_WIN)[None, :]
                in_band = jnp.where(qi < CHUNK, kj < BAND_PAST + CHUNK, kj >= CHUNK)
                bias = jnp.where(in_band[None], _rel_bias(w["attn_rel_bias"][j], Q_BLK, KEY_WIN, BAND_PAST), NEG_INF)
                att = _attn_prompt(qkv, bias, b, t)
                ks.append(kv3[:, t - rows:, d:2 * d].reshape(b, rows, h, ATTN_HEAD_DIM))
                vs.append(kv3[:, t - rows:, 2 * d:].reshape(b, rows, h, ATTN_HEAD_DIM))
            else:
                rows = cache_k.shape[2]
                win = -(-(rows + t) // LANES) * LANES
                att = _attn_sample(qkv, cache_k, cache_v, j, _rel_bias(w["attn_rel_bias"][j], t, win, rows), b, t)
                ks.append(kv3[:, :, d:2 * d].reshape(b, t, h, ATTN_HEAD_DIM))
                vs.append(kv3[:, :, 2 * d:].reshape(b, t, h, ATTN_HEAD_DIM))
            z = _matmul(att, w["attn_w_o"], (j,), F32, "attn_o", residual=(xf, alpha, 1.0))
        xf, xb = _layer_norm(z, w["ln_g"][i, 1], w["ln_b"][i, 1])
        xf, xb = ffn(xf, xb, i, 1, 2)
        xf, xb = _ple(xf, xb, p, w["ple_w_gate"], w["ple_w_proj"], (i,))
    return (xf.reshape(b, t, d), jnp.stack(shifts, axis=1), jnp.stack(wkvs, axis=1),
            jnp.stack(ks, axis=1), jnp.stack(vs, axis=1))


def _prepare_weights(ln_g, ln_b, ffn_w_gate, ffn_w_up, ffn_w_down, ple_w_gate, ple_w_proj,
                     rwkv_mu, rwkv_w_r, rwkv_w_k, rwkv_w_v, rwkv_w_o, rwkv_w0, rwkv_w1, rwkv_w2,
                     rwkv_a0, rwkv_a1, rwkv_a2, rwkv_v0, rwkv_v1, rwkv_v2, rwkv_g1, rwkv_g2,
                     rwkv_k_k, rwkv_k_a, rwkv_r_k, rwkv_ln_w, rwkv_ln_b,
                     attn_w_qkv, attn_w_o, attn_rel_bias):
    n_rwkv = rwkv_mu.shape[0]
    d = ln_g.shape[-1]
    cast = lambda a: a.astype(BF16)
    v0 = jnp.concatenate([jnp.zeros((1, d), F32), rwkv_v0], axis=0)
    return dict(
        ln_g=ln_g, ln_b=ln_b,
        ffn_gate=cast(ffn_w_gate), ffn_up=cast(ffn_w_up), ffn_down=cast(ffn_w_down),
        ple_w_gate=cast(ple_w_gate), ple_w_proj=cast(ple_w_proj),
        rwkv_mu=rwkv_mu,
        rwkv_w_r=cast(rwkv_w_r), rwkv_w_k=cast(rwkv_w_k), rwkv_w_v=cast(rwkv_w_v), rwkv_w_o=cast(rwkv_w_o),
        rwkv_w12=_pad_lora(rwkv_w1, rwkv_w2), rwkv_a12=_pad_lora(rwkv_a1, rwkv_a2),
        rwkv_v12=_pad_lora(rwkv_v1, rwkv_v2), rwkv_g12=_pad_lora(rwkv_g1, rwkv_g2),
        rwkv_prm=jnp.stack([rwkv_k_k, rwkv_k_a, rwkv_r_k.reshape(n_rwkv, d), rwkv_ln_w, rwkv_ln_b,
                            rwkv_w0, rwkv_a0, v0], axis=1),
        attn_w_qkv=cast(attn_w_qkv), attn_w_o=cast(attn_w_o), attn_rel_bias=attn_rel_bias,
    )


def kernel(x_prompt, x_sample, state_shift, state_wkv, cache_k, cache_v, p_prompt, p_sample,
           ln_g, ln_b, ffn_w_gate, ffn_w_up, ffn_w_down, ple_w_gate, ple_w_proj,
           rwkv_mu, rwkv_w_r, rwkv_w_k, rwkv_w_v, rwkv_w_o, rwkv_w0, rwkv_w1, rwkv_w2,
           rwkv_a0, rwkv_a1, rwkv_a2, rwkv_v0, rwkv_v1, rwkv_v2, rwkv_g1, rwkv_g2,
           rwkv_k_k, rwkv_k_a, rwkv_r_k, rwkv_ln_w, rwkv_ln_b,
           attn_w_qkv, attn_w_o, attn_rel_bias):
    w = _prepare_weights(ln_g, ln_b, ffn_w_gate, ffn_w_up, ffn_w_down, ple_w_gate, ple_w_proj,
                         rwkv_mu, rwkv_w_r, rwkv_w_k, rwkv_w_v, rwkv_w_o, rwkv_w0, rwkv_w1, rwkv_w2,
                         rwkv_a0, rwkv_a1, rwkv_a2, rwkv_v0, rwkv_v1, rwkv_v2, rwkv_g1, rwkv_g2,
                         rwkv_k_k, rwkv_k_a, rwkv_r_k, rwkv_ln_w, rwkv_ln_b,
                         attn_w_qkv, attn_w_o, attn_rel_bias)
    bp, _, d = x_prompt.shape
    n_rwkv = rwkv_mu.shape[0]
    heads = d // RWKV_HEAD
    zero_shift = jnp.zeros((bp, n_rwkv, d), x_prompt.dtype)
    zero_wkv = jnp.zeros((bp, n_rwkv, heads, RWKV_HEAD, RWKV_HEAD), x_prompt.dtype)
    y_p, shift_p, wkv_p, k_p, v_p = _trunk(x_prompt, p_prompt, zero_shift, zero_wkv, None, None, w, True)
    y_s, shift_s, wkv_s, k_s, v_s = _trunk(x_sample, p_sample, state_shift, state_wkv, cache_k, cache_v, w, False)
    return (y_p, y_s, shift_p, wkv_p, k_p, v_p, shift_s, wkv_s, k_s, v_s)
```

```python
import functools
import math

import jax
import jax.numpy as jnp
from jax import lax
from jax.experimental import pallas as pl
from jax.experimental.pallas import tpu as pltpu

F32 = jnp.float32
BF16 = jnp.bfloat16

CHUNK = 64
BAND_CHUNKS = 8
BAND_PAST = BAND_CHUNKS * CHUNK
REL_MAX = 256
ATTN_HEAD_DIM = 128
RWKV_HEAD = 64
LN_EPS = 1e-5
GN_EPS = 64e-5
NEG_INF = -1e30

LANES = 128
SUBLANES = 8
SLAB = 256
HEADS_PER_SLAB = SLAB // RWKV_HEAD
WKV_CHUNK = 64
WKV_SLABS_PER_STEP = 8
WKV_BLOCK_ELEMS = 512 * 1024
Q_BLK = 2 * CHUNK
KEY_WIN = BAND_PAST + Q_BLK
ATTN_GROUP = 8
VMEM_LIMIT = 56 * 1024 * 1024
MM_VMEM_BUDGET = 50 * 1024 * 1024


def _cparams(sem):
    return pltpu.CompilerParams(dimension_semantics=sem, vmem_limit_bytes=VMEM_LIMIT)


def _pow2_tile(m, cap):
    if m <= cap:
        return m
    t = 1 << (cap.bit_length() - 1)
    while m % t:
        t //= 2
    return t


def _lane_tile(n, cap):
    t = max(LANES, (cap // LANES) * LANES)
    while n % t:
        t -= LANES
    return t


def _dot(a, b):
    return jnp.dot(a.astype(BF16), b.astype(BF16), preferred_element_type=F32)


def _dot_nt(a, b):
    return lax.dot_general(a.astype(BF16), b.astype(BF16), (((1,), (1,)), ((), ())),
                           preferred_element_type=F32)


def _dot_tn(a, b):
    return lax.dot_general(a.astype(BF16), b.astype(BF16), (((0,), (0,)), ((), ())),
                           preferred_element_type=F32)


def _mm_kernel(x_ref, w_ref, o_ref):
    o_ref[...] = jnp.dot(x_ref[...], w_ref[...], preferred_element_type=F32).astype(o_ref.dtype)


def _mm_tiles(m, k, n, n_f32_blocks):
    tm = _pow2_tile(m, max(8, (12 << 20) // (2 * k)))
    tn = n
    while tn > LANES and (n % tn or 2 * (2 * tm * k + 2 * k * tn + 4 * n_f32_blocks * tm * tn) > MM_VMEM_BUDGET):
        tn -= LANES
    return tm, tn


def _wspec(w, lead, k, tn):
    assert w.ndim == len(lead) + 2
    return pl.BlockSpec((None,) * len(lead) + (k, tn), lambda i, j: tuple(lead) + (0, j))


def _mm_residual_kernel(x_ref, w_ref, r_ref, o_ref, *, alpha, scale):
    o_ref[...] = alpha * r_ref[...] + scale * jnp.dot(x_ref[...], w_ref[...], preferred_element_type=F32)


def _matmul(x, w, lead, out_dtype, name, residual=None):
    m, k = x.shape
    n = w.shape[-1]
    tm, tn = _mm_tiles(m, k, n, 1 if residual is None else 2)
    out_blk = pl.BlockSpec((tm, tn), lambda i, j: (i, j))
    in_specs = [pl.BlockSpec((tm, k), lambda i, j: (i, 0)), _wspec(w, lead, k, tn)]
    if residual is None:
        body, args = _mm_kernel, (x, w)
    else:
        r, alpha, scale = residual
        body, args = functools.partial(_mm_residual_kernel, alpha=alpha, scale=scale), (x, w, r)
        in_specs.append(out_blk)
    return pl.pallas_call(
        body,
        grid=(m // tm, n // tn),
        in_specs=in_specs,
        out_specs=out_blk,
        out_shape=jax.ShapeDtypeStruct((m, n), out_dtype),
        compiler_params=_cparams(("parallel", "arbitrary")),
        name=name,
    )(*args)


def _ffn_up_kernel(x_ref, wg_ref, wu_ref, o_ref):
    x = x_ref[...]
    g = jnp.dot(x, wg_ref[...], preferred_element_type=F32)
    u = jnp.dot(x, wu_ref[...], preferred_element_type=F32)
    o_ref[...] = (g * jax.nn.sigmoid(g) * u).astype(o_ref.dtype)


def _ffn_up(xb, wg, wu, lead):
    m, k = xb.shape
    n = wg.shape[-1]
    tm = _pow2_tile(m, max(8, (16 << 20) // (2 * k)))
    tn = _lane_tile(n, max(LANES, (2 << 20) // (2 * k)))
    return pl.pallas_call(
        _ffn_up_kernel,
        grid=(m // tm, n // tn),
        in_specs=[pl.BlockSpec((tm, k), lambda i, j: (i, 0)), _wspec(wg, lead, k, tn), _wspec(wu, lead, k, tn)],
        out_specs=pl.BlockSpec((tm, tn), lambda i, j: (i, j)),
        out_shape=jax.ShapeDtypeStruct((m, n), BF16),
        compiler_params=_cparams(("parallel", "arbitrary")),
        name="ffn_up",
    )(xb, wg, wu)


def _ln_kernel(z_ref, g_ref, b_ref, of_ref, *maybe_ob_ref):
    z = z_ref[...]
    mu = jnp.mean(z, axis=-1, keepdims=True)
    zc = z - mu
    var = jnp.mean(zc * zc, axis=-1, keepdims=True)
    o = zc * lax.rsqrt(var + LN_EPS) * g_ref[...] + b_ref[...]
    of_ref[...] = o
    for ob_ref in maybe_ob_ref:
        ob_ref[...] = o.astype(BF16)


def _layer_norm(z, g, b, with_bf16=True):
    m, d = z.shape
    tr = _pow2_tile(m, max(8, (8 << 20) // (4 * d)))
    row = pl.BlockSpec((tr, d), lambda i: (i, 0))
    vec = pl.BlockSpec((1, d), lambda i: (0, 0))
    n_out = 2 if with_bf16 else 1
    outs = pl.pallas_call(
        _ln_kernel,
        grid=(m // tr,),
        in_specs=[row, vec, vec],
        out_specs=[row] * n_out,
        out_shape=[jax.ShapeDtypeStruct((m, d), F32), jax.ShapeDtypeStruct((m, d), BF16)][:n_out],
        compiler_params=_cparams(("parallel",)),
        name="layer_norm",
    )(z, g.reshape(1, d), b.reshape(1, d))
    return (outs[0], outs[1]) if with_bf16 else (outs[0], None)


def _ple_kernel(xb_ref, wg_ref, p_ref, wp_ref, x_ref, of_ref, ob_ref):
    gate = jax.nn.sigmoid(jnp.dot(xb_ref[...], wg_ref[...], preferred_element_type=F32))
    emb = jnp.dot(p_ref[...].astype(BF16), wp_ref[...], preferred_element_type=F32)
    o = x_ref[...] + gate * emb
    of_ref[...] = o
    ob_ref[...] = o.astype(BF16)


def _ple(x, xb, p, wg, wp, lead):
    m, d = x.shape
    pd = p.shape[-1]
    tm = _pow2_tile(m, max(8, (8 << 20) // (2 * d)))
    tn = _lane_tile(d, max(LANES, (4 << 20) // (2 * d)))
    blk = pl.BlockSpec((tm, tn), lambda i, j: (i, j))
    return pl.pallas_call(
        _ple_kernel,
        grid=(m // tm, d // tn),
        in_specs=[pl.BlockSpec((tm, d), lambda i, j: (i, 0)), _wspec(wg, lead, d, tn),
                  pl.BlockSpec((None, tm, pd), lambda i, j: tuple(lead) + (i, 0)), _wspec(wp, lead, pd, tn), blk],
        out_specs=[blk, blk],
        out_shape=[jax.ShapeDtypeStruct((m, d), F32), jax.ShapeDtypeStruct((m, d), BF16)],
        compiler_params=_cparams(("parallel", "arbitrary")),
        name="ple",
    )(xb, wg, p, wp, x)


def _mix_kernel(x_ref, prev_ref, mu_ref, w1_ref, a1_ref, g1_ref, *rest, with_v):
    if with_v:
        v1_ref, xr_ref, xk_ref, xv_ref, tw_ref, ta_ref, tg_ref, tv_ref = rest
    else:
        xr_ref, xk_ref, xv_ref, tw_ref, ta_ref, tg_ref = rest
    x = x_ref[...]
    shifted = pltpu.roll(x, 1, 0)
    first = lax.broadcasted_iota(jnp.int32, x.shape, 0) == 0
    shifted = jnp.where(first, prev_ref[...], shifted)
    xx = shifted - x
    mix = lambda m: (x + xx * mu_ref[m:m + 1, :]).astype(BF16)
    down = lambda xm, w_ref: jnp.dot(xm, w_ref[...], preferred_element_type=F32)
    xr_ref[...] = mix(0)
    tw_ref[...] = jnp.tanh(down(mix(1), w1_ref)).astype(BF16)
    xk_ref[...] = mix(2)
    xv = mix(3)
    xv_ref[...] = xv
    ta_ref[...] = down(mix(4), a1_ref).astype(BF16)
    tg_ref[...] = jax.nn.sigmoid(down(mix(5), g1_ref)).astype(BF16)
    if with_v:
        tv_ref[...] = down(xv, v1_ref).astype(BF16)


def _token_mixes(x, x_prev, mu, w1, a1, g1, v1, layer):
    b, t, d = x.shape
    tt = _pow2_tile(t, max(8, (4 << 20) // (4 * d)))
    nt = t // tt
    prev = jnp.concatenate([x_prev[:, None, :], x[:, tt - 1:t - 1:tt, :]], axis=1).reshape(b, nt, 1, d)
    blk = pl.BlockSpec((None, tt, d), lambda i, j: (i, j, 0))
    wspec = lambda w, idx: pl.BlockSpec((None, d, w.shape[-1]), lambda i, j: (idx, 0, 0))
    tspec = lambda w: pl.BlockSpec((None, tt, w.shape[-1]), lambda i, j: (i, j, 0))
    tshape = lambda w: jax.ShapeDtypeStruct((b, t, w.shape[-1]), BF16)
    lows = [(w1, layer), (a1, layer), (g1, layer)] + ([(v1, layer - 1)] if layer > 0 else [])
    outs = pl.pallas_call(
        functools.partial(_mix_kernel, with_v=layer > 0),
        grid=(b, nt),
        in_specs=[blk,
                  pl.BlockSpec((None, None, 1, d), lambda i, j: (i, j, 0, 0)),
                  pl.BlockSpec((6, d), lambda i, j: (0, 0))] + [wspec(w, idx) for w, idx in lows],
        out_specs=[blk] * 3 + [tspec(w) for w, _ in lows],
        out_shape=[jax.ShapeDtypeStruct((b, t, d), BF16)] * 3 + [tshape(w) for w, _ in lows],
        compiler_params=_cparams(("parallel", "parallel")),
        name="token_mix",
    )(x, prev, mu, *[w for w, _ in lows])
    return [o.reshape(b * t, d) for o in outs[:3]], list(outs[3:])


def _pad_lora(w1, w2):
    r = w1.shape[-1]
    rp = -(-r // LANES) * LANES
    w1 = jnp.pad(w1.astype(BF16), ((0, 0), (0, 0), (0, rp - r)))
    w2 = jnp.pad(w2.astype(BF16), ((0, 0), (0, rp - r), (0, 0)))
    return w1, w2


def _round_robin(gens, batched=None):
    replies = [None] * len(gens)
    while gens:
        alive, asks = [], []
        for g, reply in zip(gens, replies):
            try:
                asks.append(g.send(reply))
                alive.append(g)
            except StopIteration:
                pass
        gens = alive
        replies = batched(asks) if asks and asks[0] is not None else [None] * len(gens)


def _split_bf16(x):
    hi = x.astype(BF16)
    lo = (x - hi.astype(F32)).astype(BF16)
    return hi, lo


def _wkv_kernel(*refs, chunk, n_chunks, n_slabs, with_v, t_valid):
    if with_v:
        (r_ref, k_ref, v_ref, vf_ref, tw_ref, ta_ref, tg_ref, tv_ref, w2_ref, a2_ref, g2_ref, v2_ref,
         prm_ref, s0_ref, o_ref, sf_ref, s_scr) = refs
    else:
        (r_ref, k_ref, v_ref, tw_ref, ta_ref, tg_ref, w2_ref, a2_ref, g2_ref,
         prm_ref, s0_ref, o_ref, sf_ref, s_scr) = refs
    c = chunk
    rows = HEADS_PER_SLAB * c
    tstep = pl.program_id(2)
    t_block = n_chunks * c

    lane = lax.broadcasted_iota(jnp.int32, (1, SLAB), 1)
    head_masks = [(lane // RWKV_HEAD) == j for j in range(HEADS_PER_SLAB)]
    ri = lax.broadcasted_iota(jnp.int32, (rows, rows), 0)
    ci = lax.broadcasted_iota(jnp.int32, (rows, rows), 1)
    tri_strict = (ri % c) > (ci % c)
    tri_incl_cat = (lax.broadcasted_iota(jnp.int32, (c, rows), 0)
                    >= lax.broadcasted_iota(jnp.int32, (c, rows), 1) % c)
    eye = jnp.where(ri == ci, 1.0, 0.0).astype(F32)
    rs = lax.broadcasted_iota(jnp.int32, (SLAB, SLAB), 0)
    cs = lax.broadcasted_iota(jnp.int32, (SLAB, SLAB), 1)
    same_head = (rs // RWKV_HEAD) == (cs // RWKV_HEAD)
    ones_bd = jnp.where(same_head, 1.0, 0.0).astype(BF16)
    lt_incl = jnp.where(lax.broadcasted_iota(jnp.int32, (c, c), 0) >= lax.broadcasted_iota(jnp.int32, (c, c), 1),
                        1.0, 0.0).astype(BF16)

    @pl.when(tstep == 0)
    def _():
        for q in range(n_slabs):
            s0 = s0_ref[q]
            s_scr[q] = jnp.where(same_head, jnp.concatenate([s0] * HEADS_PER_SLAB, axis=0), 0.0)

    def stack(x):
        return jnp.concatenate([jnp.where(m, x, 0.0) for m in head_masks], axis=0)

    def fold(y):
        out = y[0:c]
        for j in range(1, HEADS_PER_SLAB):
            out = out + y[j * c:(j + 1) * c]
        return out

    def skip_rows(m):
        return (min(m, c) // SUBLANES) * SUBLANES

    def head_rows(x, lo):
        if lo == 0:
            return x
        return jnp.concatenate([x[j * c + lo:(j + 1) * c] for j in range(HEADS_PER_SLAB)], axis=0)

    def head_rows_back(x, lo):
        if lo == 0:
            return x
        keep = c - lo
        zeros = jnp.zeros((lo, x.shape[1]), x.dtype)
        return jnp.concatenate([blk for j in range(HEADS_PER_SLAB)
                                for blk in (zeros, x[j * keep:(j + 1) * keep])], axis=0)

    def head_sums(asks):
        flat = [x.astype(BF16) for xs in asks for x in xs]
        tot = jnp.dot(jnp.concatenate(flat, axis=0), ones_bd, preferred_element_type=F32)
        parts = iter(tot[i * c:(i + 1) * c] for i in range(len(flat)))
        return [[next(parts) for _ in xs] for xs in asks]

    n_sq = int(math.log2(c)) - 1

    def chunk_step(q, start):
        sl = pl.ds(start, c)
        lanes = slice(q * SLAB, (q + 1) * SLAB)
        prm = prm_ref[:, lanes]
        k_k, k_a, r_k, ln_w, ln_b, w0, a0, v0 = (prm[i:i + 1, :] for i in range(8))
        up = lambda t_ref, w_ref: jnp.dot(t_ref[sl, :], w_ref[:, lanes], preferred_element_type=F32)
        r = r_ref[sl, lanes]
        k = k_ref[sl, lanes]
        v = v_ref[sl, lanes]
        u = -(w0 + up(tw_ref, w2_ref))
        e = jnp.exp(-(jnp.maximum(u, 0.0) + jnp.log(1.0 + jnp.exp(-jnp.abs(u)))) - 0.5)
        if t_valid is not None:
            frame = tstep * t_block + start + lax.broadcasted_iota(jnp.int32, (c, SLAB), 0)
            e = jnp.where(frame < t_valid, e, 0.0)
        a = jax.nn.sigmoid(a0 + up(ta_ref, a2_ref))
        g = up(tg_ref, g2_ref)
        if with_v:
            v = v + (vf_ref[sl, lanes] - v) * jax.nn.sigmoid(v0 + up(tv_ref, v2_ref))
        yield

        kk = k * k_k
        k2 = k * (1.0 + (a - 1.0) * k_a)
        ss, bonus = yield [kk * kk, r * k2 * r_k]
        kkn = kk / jnp.maximum(jnp.sqrt(ss), 1e-12)
        av = -kkn
        bv = kkn * a

        e_hi, e_lo = _split_bf16(e)
        cum = (jnp.dot(lt_incl, e_hi, preferred_element_type=F32)
               + jnp.dot(lt_incl, e_lo, preferred_element_type=F32))
        yield
        cum_last = cum[c - 1:c, :]
        p_in = jnp.exp(-cum)
        p_ex = jnp.exp(e - cum)
        p_inv = jnp.exp(cum)
        p_end = jnp.exp(cum - cum_last)
        p_c = jnp.exp(-cum_last)

        b_st = stack(bv * p_inv)
        k_st = stack(k2 * p_inv)
        v_st = stack(v)
        ar = jnp.concatenate([stack(av * p_ex), r * p_in], axis=0).astype(BF16)
        xb = _dot_nt(ar, b_st)
        yield
        xk = _dot_nt(ar, k_st)
        yield
        a_ab = jnp.where(tri_strict, xb[:rows], 0.0)
        a_rb = jnp.where(tri_incl_cat, xb[rows:], 0.0)
        a_ak = jnp.where(tri_strict, xk[:rows], 0.0)
        a_rk = jnp.where(tri_incl_cat, xk[rows:], 0.0)

        x_pow = _dot(a_ab, a_ab)
        akv = _dot(a_ak, v_st)
        yield
        t_inv = eye + a_ab
        m = 2
        for _ in range(n_sq - 1):
            lo_t, lo_x = skip_rows(m), skip_rows(2 * m)
            both = _dot(jnp.concatenate([head_rows(t_inv, lo_t), head_rows(x_pow, lo_x)], axis=0), x_pow)
            yield
            n_t = HEADS_PER_SLAB * (c - lo_t)
            t_inv = t_inv + head_rows_back(both[:n_t], lo_t)
            x_pow = head_rows_back(both[n_t:], lo_x)
            m *= 2
        lo_t = skip_rows(m)
        t_inv = t_inv + head_rows_back(_dot(head_rows(t_inv, lo_t), x_pow), lo_t)
        yield

        s_bd = s_scr[q]
        ars = _dot_nt(ar, s_bd)
        yield
        u_st = _dot(t_inv, ars[:rows] + akv)
        yield
        y = ars[rows:] + _dot(a_rb, u_st) + _dot(a_rk, v_st)
        yield
        uv = jnp.concatenate([fold(u_st), v], axis=0)
        bk = jnp.concatenate([bv * p_end, k2 * p_end], axis=0)
        s_scr[q] = s_bd * p_c + jnp.where(same_head, _dot_tn(uv, bk), 0.0)
        yield

        mean_hi, mean_lo = yield list(_split_bf16(y))
        yc = y - (mean_hi + mean_lo) * (1.0 / RWKV_HEAD)
        (var,) = yield [yc * yc]
        yn = yc * lax.rsqrt(var * (1.0 / RWKV_HEAD) + GN_EPS) * ln_w + ln_b
        o_ref[sl, lanes] = ((yn + bonus * v) * g).astype(o_ref.dtype)

    def body(idx, carry):
        start = pl.multiple_of(idx * c, c)
        _round_robin([chunk_step(q, start) for q in range(n_slabs)], batched=head_sums)
        return carry

    lax.fori_loop(0, n_chunks, body, 0)

    @pl.when(tstep == pl.num_programs(2) - 1)
    def _():
        for q in range(n_slabs):
            s_bd = s_scr[q]
            out = s_bd[0:RWKV_HEAD]
            for j in range(1, HEADS_PER_SLAB):
                out = out + s_bd[j * RWKV_HEAD:(j + 1) * RWKV_HEAD]
            sf_ref[q] = out


def _wkv(r, k, v, v_first, lows, ups, prm, layer, s0):
    b, t_in, d = r.shape
    n = RWKV_HEAD
    nslab = d // SLAB
    c = WKV_CHUNK
    t = -(-t_in // c) * c
    with_v = v_first is not None
    seqs = [r, k, v] + ([v_first] if with_v else [])
    if t != t_in:
        pad = lambda z: jnp.pad(z, ((0, 0), (0, t - t_in), (0, 0)))
        seqs, lows = [pad(z) for z in seqs], [pad(z) for z in lows]
    ns = max(s for s in (1, 2, 4, 8) if s <= WKV_SLABS_PER_STEP and nslab % s == 0)
    tb = _pow2_tile(t, max(c, WKV_BLOCK_ELEMS // (ns * SLAB)))
    s0_slab = s0.reshape(b, nslab, HEADS_PER_SLAB, n, n).transpose(0, 1, 3, 2, 4).reshape(b, nslab, n, SLAB)
    seq = pl.BlockSpec((None, tb, ns * SLAB), lambda i, s, j: (i, j, s))
    low = lambda z: pl.BlockSpec((None, tb, z.shape[-1]), lambda i, s, j: (i, j, 0))
    up = lambda w, idx: pl.BlockSpec((None, w.shape[1], ns * SLAB), lambda i, s, j: (idx, 0, s))
    st = pl.BlockSpec((None, ns, n, SLAB), lambda i, s, j: (i, s, 0, 0))
    out, sf = pl.pallas_call(
        functools.partial(_wkv_kernel, chunk=c, n_chunks=tb // c, n_slabs=ns, with_v=with_v,
                          t_valid=t_in if t != t_in else None),
        grid=(b, nslab // ns, t // tb),
        in_specs=([seq] * len(seqs) + [low(z) for z in lows] + [up(w, idx) for w, idx in ups]
                  + [pl.BlockSpec((None, 8, ns * SLAB), lambda i, s, j: (layer, 0, s)), st]),
        out_specs=[seq, st],
        out_shape=[jax.ShapeDtypeStruct((b, t, d), BF16), jax.ShapeDtypeStruct((b, nslab, n, SLAB), F32)],
        scratch_shapes=[pltpu.VMEM((ns, SLAB, SLAB), F32)],
        compiler_params=_cparams(("parallel", "parallel", "arbitrary")),
        name="wkv7",
    )(*seqs, *lows, *[w for w, _ in ups], prm, s0_slab)
    sf = sf.reshape(b, nslab, n, HEADS_PER_SLAB, n).transpose(0, 1, 3, 2, 4).reshape(b, d // n, n, n)
    return out[:, :t_in], sf


def _attn_prompt_kernel(q_ref, k_ref, v_ref, bias_ref, o_ref, kp, vp, *, t, scale):
    dh = kp.shape[1]
    zeros_past = jnp.zeros((BAND_PAST, dh), BF16)
    kp[0:BAND_PAST, :] = zeros_past
    vp[0:BAND_PAST, :] = zeros_past
    kp[BAND_PAST:, :] = k_ref[...].astype(BF16)
    vp[BAND_PAST:, :] = v_ref[...].astype(BF16)
    col = lax.broadcasted_iota(jnp.int32, (Q_BLK, KEY_WIN), 1)

    def block(c, mask_start):
        start = pl.multiple_of(c * Q_BLK, Q_BLK)
        s = _dot_nt(q_ref[pl.ds(start, Q_BLK), :], kp[pl.ds(start, KEY_WIN), :])
        yield
        s = s * scale + bias_ref[...]
        if mask_start:
            s = jnp.where(col >= BAND_PAST - c * Q_BLK, s, NEG_INF)
        m = jnp.max(s, axis=-1, keepdims=True)
        p = jnp.exp(s - m)
        p = p / jnp.sum(p, axis=-1, keepdims=True)
        o = _dot(p, vp[pl.ds(start, KEY_WIN), :])
        yield
        o_ref[pl.ds(start, Q_BLK), :] = o.astype(o_ref.dtype)

    def make_body(mask_start):
        def body(g, carry):
            _round_robin([block(g * ATTN_GROUP + i, mask_start) for i in range(ATTN_GROUP)])
            return carry
        return body

    n_grp = t // (Q_BLK * ATTN_GROUP)
    n_front = min(-(-BAND_PAST // (Q_BLK * ATTN_GROUP)), n_grp)
    lax.fori_loop(0, n_front, make_body(True), 0)
    lax.fori_loop(n_front, n_grp, make_body(False), 0)


def _attn_prompt(qkv, bias, b, t):
    d = qkv.shape[1] // 3
    h = d // ATTN_HEAD_DIM
    assert t % (ATTN_GROUP * Q_BLK) == 0
    col = lambda off: pl.BlockSpec((t, ATTN_HEAD_DIM), lambda i, j: (i, off + j))
    return pl.pallas_call(
        functools.partial(_attn_prompt_kernel, t=t, scale=ATTN_HEAD_DIM ** -0.5),
        grid=(b, h),
        in_specs=[col(0), col(h), col(2 * h),
                  pl.BlockSpec((None, Q_BLK, KEY_WIN), lambda i, j: (j, 0, 0))],
        out_specs=col(0),
        out_shape=jax.ShapeDtypeStruct((b * t, d), BF16),
        scratch_shapes=[pltpu.VMEM((BAND_PAST + t, ATTN_HEAD_DIM), BF16)] * 2,
        compiler_params=_cparams(("parallel", "parallel")),
        name="band_attn_prompt",
    )(qkv, qkv, qkv, bias)


def _attn_sample_kernel(q_ref, kn_ref, vn_ref, kc_ref, vc_ref, bias_ref, o_ref, kf, vf, *, rows, heads, scale):
    t = q_ref.shape[0]
    dh = ATTN_HEAD_DIM
    group = kf.shape[0]
    pad = jnp.zeros((kf.shape[1] - rows - t, dh), BF16)

    def head(h, slot):
        ln = slice(h * dh, (h + 1) * dh)
        for buf, cache_ref, new_ref in ((kf, kc_ref, kn_ref), (vf, vc_ref, vn_ref)):
            buf[slot, 0:rows, :] = cache_ref[pl.ds(h, rows, stride=heads), :].astype(BF16)
            buf[slot, rows:rows + t, :] = new_ref[:, ln].astype(BF16)
            buf[slot, rows + t:, :] = pad
        s = _dot_nt(q_ref[:, ln], kf[slot])
        yield
        s = s * scale + bias_ref[h]
        col = lax.broadcasted_iota(jnp.int32, s.shape, 1)
        s = jnp.where(col < rows + t, s, NEG_INF)
        m = jnp.max(s, axis=-1, keepdims=True)
        p = jnp.exp(s - m)
        p = p / jnp.sum(p, axis=-1, keepdims=True)
        o = _dot(p, vf[slot])
        yield
        o_ref[:, ln] = o.astype(o_ref.dtype)

    for h0 in range(0, heads, group):
        _round_robin([head(h0 + i, i) for i in range(group)])


def _attn_sample(qkv, cache_k, cache_v, layer, bias, b, t):
    d = qkv.shape[1] // 3
    h = d // ATTN_HEAD_DIM
    rows = cache_k.shape[2]
    win = bias.shape[2]
    group = max(s for s in (1, 2, 4) if h % s == 0)
    flat = lambda c: c.reshape(c.shape[0], c.shape[1], rows * h, ATTN_HEAD_DIM)
    col = lambda off: pl.BlockSpec((t, d), lambda i: (i, off))
    cache = pl.BlockSpec((None, None, rows * h, ATTN_HEAD_DIM), lambda i: (i, layer, 0, 0))
    return pl.pallas_call(
        functools.partial(_attn_sample_kernel, rows=rows, heads=h, scale=ATTN_HEAD_DIM ** -0.5),
        grid=(b,),
        in_specs=[col(0), col(1), col(2), cache, cache, pl.BlockSpec((h, t, win), lambda i: (0, 0, 0))],
        out_specs=col(0),
        out_shape=jax.ShapeDtypeStruct((b * t, d), BF16),
        scratch_shapes=[pltpu.VMEM((group, win, ATTN_HEAD_DIM), BF16)] * 2,
        compiler_params=_cparams(("parallel",)),
        name="band_attn_sample",
    )(qkv, qkv, qkv, flat(cache_k), flat(cache_v), bias)


def _rel_bias(table, n_q, n_k, offset):
    span = n_q + n_k - 1
    dist = (n_q - 1) - jnp.arange(span) + offset
    rev = table[:, jnp.clip(dist, -(CHUNK - 1), REL_MAX) + (CHUNK - 1)].astype(F32)
    period = jnp.pad(rev, ((0, 0), (0, 1)))
    shifted = jnp.tile(period, (1, n_q))[:, :n_q * span].reshape(table.shape[0], n_q, span)
    return shifted[:, :, n_q - 1:n_q - 1 + n_k]


def _trunk(x, p, shift0, wkv0, cache_k, cache_v, w, is_prompt):
    b, t, d = x.shape
    m = b * t
    depth = w["ln_g"].shape[0]
    alpha = (2 * depth) ** 0.25
    xf = x.reshape(m, d)
    xb = xf.astype(BF16)
    p = p.reshape(depth, m, p.shape[-1])
    v_first = None
    shifts, wkvs, ks, vs = [], [], [], []

    def ffn(xf, xb, i, s, ln_idx, with_bf16=True):
        hidden = _ffn_up(xb, w["ffn_gate"], w["ffn_up"], (i, s))
        z = _matmul(hidden, w["ffn_down"], (i, s), F32, "ffn_down", residual=(xf, alpha, 0.5))
        return _layer_norm(z, w["ln_g"][i, ln_idx], w["ln_b"][i, ln_idx], with_bf16)

    for i in range(depth):
        xf, xb = ffn(xf, xb, i, 0, 0, with_bf16=(i % 2 == 1))
        j = i // 2
        if i % 2 == 0:
            x3 = xf.reshape(b, t, d)
            (w1, w2), (a1, a2), (g1, g2), (v1, v2) = (w[n] for n in ("rwkv_w12", "rwkv_a12", "rwkv_g12", "rwkv_v12"))
            (xr, xk, xv), lows = _token_mixes(x3, shift0[:, j], w["rwkv_mu"][j], w1, a1, g1, v1, j)
            shifts.append(x3[:, -1])
            to3 = lambda z: z.reshape(b, t, d)
            r = to3(_matmul(xr, w["rwkv_w_r"], (j,), F32, "rwkv_r"))
            k = to3(_matmul(xk, w["rwkv_w_k"], (j,), F32, "rwkv_k"))
            v = to3(_matmul(xv, w["rwkv_w_v"], (j,), F32, "rwkv_v"))
            ups = [(w2, j), (a2, j), (g2, j)] + ([(v2, j - 1)] if j > 0 else [])
            mix, s_fin = _wkv(r, k, v, v_first if j > 0 else None, lows, ups, w["rwkv_prm"], j, wkv0[:, j])
            if j == 0:
                v_first = v
            wkvs.append(s_fin)
            z = _matmul(mix.reshape(m, d), w["rwkv_w_o"], (j,), F32, "rwkv_o", residual=(xf, alpha, 1.0))
        else:
            qkv = _matmul(xb, w["attn_w_qkv"], (j,), F32, "attn_qkv")
            kv3 = qkv.reshape(b, t, 3 * d)
            h = d // ATTN_HEAD_DIM
            if is_prompt:
                rows = min(BAND_PAST, t)
                qi = jnp.arange(Q_BLK)[:, None]
                kj = jnp.arange(KEY_WIN)[None, :]
                in_band = jnp.where(qi < CHUNK, kj < BAND_PAST + CHUNK, kj >= CHUNK)
                bias = jnp.where(in_band[None], _rel_bias(w["attn_rel_bias"][j], Q_BLK, KEY_WIN, BAND_PAST), NEG_INF)
                att = _attn_prompt(qkv, bias, b, t)
                ks.append(kv3[:, t - rows:, d:2 * d].reshape(b, rows, h, ATTN_HEAD_DIM))
                vs.append(kv3[:, t - rows:, 2 * d:].reshape(b, rows, h, ATTN_HEAD_DIM))
            else:
                rows = cache_k.shape[2]
                win = -(-(rows + t) // LANES) * LANES
                att = _attn_sample(qkv, cache_k, cache_v, j, _rel_bias(w["attn_rel_bias"][j], t, win, rows), b, t)
                ks.append(kv3[:, :, d:2 * d].reshape(b, t, h, ATTN_HEAD_DIM))
                vs.append(kv3[:, :, 2 * d:].reshape(b, t, h, ATTN_HEAD_DIM))
            z = _matmul(att, w["attn_w_o"], (j,), F32, "attn_o", residual=(xf, alpha, 1.0))
        xf, xb = _layer_norm(z, w["ln_g"][i, 1], w["ln_b"][i, 1])
        xf, xb = ffn(xf, xb, i, 1, 2)
        xf, xb = _ple(xf, xb, p, w["ple_w_gate"], w["ple_w_proj"], (i,))
    return (xf.reshape(b, t, d), jnp.stack(shifts, axis=1), jnp.stack(wkvs, axis=1),
            jnp.stack(ks, axis=1), jnp.stack(vs, axis=1))


def _prepare_weights(ln_g, ln_b, ffn_w_gate, ffn_w_up, ffn_w_down, ple_w_gate, ple_w_proj,
                     rwkv_mu, rwkv_w_r, rwkv_w_k, rwkv_w_v, rwkv_w_o, rwkv_w0, rwkv_w1, rwkv_w2,
                     rwkv_a0, rwkv_a1, rwkv_a2, rwkv_v0, rwkv_v1, rwkv_v2, rwkv_g1, rwkv_g2,
                     rwkv_k_k, rwkv_k_a, rwkv_r_k, rwkv_ln_w, rwkv_ln_b,
                     attn_w_qkv, attn_w_o, attn_rel_bias):
    n_rwkv = rwkv_mu.shape[0]
    d = ln_g.shape[-1]
    cast = lambda a: a.astype(BF16)
    v0 = jnp.concatenate([jnp.zeros((1, d), F32), rwkv_v0], axis=0)
    return dict(
        ln_g=ln_g, ln_b=ln_b,
        ffn_gate=cast(ffn_w_gate), ffn_up=cast(ffn_w_up), ffn_down=cast(ffn_w_down),
        ple_w_gate=cast(ple_w_gate), ple_w_proj=cast(ple_w_proj),
        rwkv_mu=rwkv_mu,
        rwkv_w_r=cast(rwkv_w_r), rwkv_w_k=cast(rwkv_w_k), rwkv_w_v=cast(rwkv_w_v), rwkv_w_o=cast(rwkv_w_o),
        rwkv_w12=_pad_lora(rwkv_w1, rwkv_w2), rwkv_a12=_pad_lora(rwkv_a1, rwkv_a2),
        rwkv_v12=_pad_lora(rwkv_v1, rwkv_v2), rwkv_g12=_pad_lora(rwkv_g1, rwkv_g2),
        rwkv_prm=jnp.stack([rwkv_k_k, rwkv_k_a, rwkv_r_k.reshape(n_rwkv, d), rwkv_ln_w, rwkv_ln_b,
                            rwkv_w0, rwkv_a0, v0], axis=1),
        attn_w_qkv=cast(attn_w_qkv), attn_w_o=cast(attn_w_o), attn_rel_bias=attn_rel_bias,
    )


def kernel(x_prompt, x_sample, state_shift, state_wkv, cache_k, cache_v, p_prompt, p_sample,
           ln_g, ln_b, ffn_w_gate, ffn_w_up, ffn_w_down, ple_w_gate, ple_w_proj,
           rwkv_mu, rwkv_w_r, rwkv_w_k, rwkv_w_v, rwkv_w_o, rwkv_w0, rwkv_w1, rwkv_w2,
           rwkv_a0, rwkv_a1, rwkv_a2, rwkv_v0, rwkv_v1, rwkv_v2, rwkv_g1, rwkv_g2,
           rwkv_k_k, rwkv_k_a, rwkv_r_k, rwkv_ln_w, rwkv_ln_b,
           attn_w_qkv, attn_w_o, attn_rel_bias):
    w = _prepare_weights(ln_g, ln_b, ffn_w_gate, ffn_w_up, ffn_w_down, ple_w_gate, ple_w_proj,
                         rwkv_mu, rwkv_w_r, rwkv_w_k, rwkv_w_v, rwkv_w_o, rwkv_w0, rwkv_w1, rwkv_w2,
                         rwkv_a0, rwkv_a1, rwkv_a2, rwkv_v0, rwkv_v1, rwkv_v2, rwkv_g1, rwkv_g2,
                         rwkv_k_k, rwkv_k_a, rwkv_r_k, rwkv_ln_w, rwkv_ln_b,
                         attn_w_qkv, attn_w_o, attn_rel_bias)
    bp, _, d = x_prompt.shape
    n_rwkv = rwkv_mu.shape[0]
    heads = d // RWKV_HEAD
    zero_shift = jnp.zeros((bp, n_rwkv, d), x_prompt.dtype)
    zero_wkv = jnp.zeros((bp, n_rwkv, heads, RWKV_HEAD, RWKV_HEAD), x_prompt.dtype)
    y_p, shift_p, wkv_p, k_p, v_p = _trunk(x_prompt, p_prompt, zero_shift, zero_wkv, None, None, w, True)
    y_s, shift_s, wkv_s, k_s, v_s = _trunk(x_sample, p_sample, state_shift, state_wkv, cache_k, cache_v, w, False)
    return (y_p, y_s, shift_p, wkv_p, k_p, v_p, shift_s, wkv_s, k_s, v_s)
```

```python
import functools
import math

import jax
import jax.numpy as jnp
from jax import lax
from jax.experimental import pallas as pl
from jax.experimental.pallas import tpu as pltpu

F32 = jnp.float32
BF16 = jnp.bfloat16

CHUNK = 64
BAND_CHUNKS = 8
BAND_PAST = BAND_CHUNKS * CHUNK
REL_MAX = 256
ATTN_HEAD_DIM = 128
RWKV_HEAD = 64
LN_EPS = 1e-5
GN_EPS = 64e-5
NEG_INF = -1e30

LANES = 128
SUBLANES = 8
SLAB = 256
HEADS_PER_SLAB = SLAB // RWKV_HEAD
WKV_CHUNK = 64
WKV_SLABS_PER_STEP = 8
WKV_BLOCK_ELEMS = 512 * 1024
Q_BLK = 2 * CHUNK
KEY_WIN = BAND_PAST + Q_BLK
ATTN_GROUP = 16
VMEM_LIMIT = 56 * 1024 * 1024
MM_VMEM_BUDGET = 50 * 1024 * 1024


def _cparams(sem):
    return pltpu.CompilerParams(dimension_semantics=sem, vmem_limit_bytes=VMEM_LIMIT)


def _pow2_tile(m, cap):
    if m <= cap:
        return m
    t = 1 << (cap.bit_length() - 1)
    while m % t:
        t //= 2
    return t


def _lane_tile(n, cap):
    t = max(LANES, (cap // LANES) * LANES)
    while n % t:
        t -= LANES
    return t


def _dot(a, b):
    return jnp.dot(a.astype(BF16), b.astype(BF16), preferred_element_type=F32)


def _dot_nt(a, b):
    return lax.dot_general(a.astype(BF16), b.astype(BF16), (((1,), (1,)), ((), ())),
                           preferred_element_type=F32)


def _dot_tn(a, b):
    return lax.dot_general(a.astype(BF16), b.astype(BF16), (((0,), (0,)), ((), ())),
                           preferred_element_type=F32)


def _mm_kernel(x_ref, w_ref, o_ref):
    o_ref[...] = jnp.dot(x_ref[...], w_ref[...], preferred_element_type=F32).astype(o_ref.dtype)


def _mm_tiles(m, k, n, n_f32_blocks):
    tm = _pow2_tile(m, max(8, (12 << 20) // (2 * k)))
    tn = n
    while tn > LANES and (n % tn or 2 * (2 * tm * k + 2 * k * tn + 4 * n_f32_blocks * tm * tn) > MM_VMEM_BUDGET):
        tn -= LANES
    return tm, tn


def _wspec(w, lead, k, tn):
    assert w.ndim == len(lead) + 2
    return pl.BlockSpec((None,) * len(lead) + (k, tn), lambda i, j: tuple(lead) + (0, j))


def _mm_residual_kernel(x_ref, w_ref, r_ref, o_ref, *, alpha, scale):
    o_ref[...] = alpha * r_ref[...] + scale * jnp.dot(x_ref[...], w_ref[...], preferred_element_type=F32)


def _matmul(x, w, lead, out_dtype, name, residual=None):
    m, k = x.shape
    n = w.shape[-1]
    tm, tn = _mm_tiles(m, k, n, 1 if residual is None else 2)
    out_blk = pl.BlockSpec((tm, tn), lambda i, j: (i, j))
    in_specs = [pl.BlockSpec((tm, k), lambda i, j: (i, 0)), _wspec(w, lead, k, tn)]
    if residual is None:
        body, args = _mm_kernel, (x, w)
    else:
        r, alpha, scale = residual
        body, args = functools.partial(_mm_residual_kernel, alpha=alpha, scale=scale), (x, w, r)
        in_specs.append(out_blk)
    return pl.pallas_call(
        body,
        grid=(m // tm, n // tn),
        in_specs=in_specs,
        out_specs=out_blk,
        out_shape=jax.ShapeDtypeStruct((m, n), out_dtype),
        compiler_params=_cparams(("parallel", "arbitrary")),
        name=name,
    )(*args)


def _ffn_up_kernel(x_ref, wg_ref, wu_ref, o_ref):
    x = x_ref[...]
    g = jnp.dot(x, wg_ref[...], preferred_element_type=F32)
    u = jnp.dot(x, wu_ref[...], preferred_element_type=F32)
    o_ref[...] = (g * jax.nn.sigmoid(g) * u).astype(o_ref.dtype)


def _ffn_up(xb, wg, wu, lead):
    m, k = xb.shape
    n = wg.shape[-1]
    tm = _pow2_tile(m, max(8, (16 << 20) // (2 * k)))
    tn = _lane_tile(n, max(LANES, (2 << 20) // (2 * k)))
    return pl.pallas_call(
        _ffn_up_kernel,
        grid=(m // tm, n // tn),
        in_specs=[pl.BlockSpec((tm, k), lambda i, j: (i, 0)), _wspec(wg, lead, k, tn), _wspec(wu, lead, k, tn)],
        out_specs=pl.BlockSpec((tm, tn), lambda i, j: (i, j)),
        out_shape=jax.ShapeDtypeStruct((m, n), BF16),
        compiler_params=_cparams(("parallel", "arbitrary")),
        name="ffn_up",
    )(xb, wg, wu)


def _ln_kernel(z_ref, g_ref, b_ref, of_ref, *maybe_ob_ref):
    z = z_ref[...]
    mu = jnp.mean(z, axis=-1, keepdims=True)
    zc = z - mu
    var = jnp.mean(zc * zc, axis=-1, keepdims=True)
    o = zc * lax.rsqrt(var + LN_EPS) * g_ref[...] + b_ref[...]
    of_ref[...] = o
    for ob_ref in maybe_ob_ref:
        ob_ref[...] = o.astype(BF16)


def _layer_norm(z, g, b, with_bf16=True):
    m, d = z.shape
    tr = _pow2_tile(m, max(8, (8 << 20) // (4 * d)))
    row = pl.BlockSpec((tr, d), lambda i: (i, 0))
    vec = pl.BlockSpec((1, d), lambda i: (0, 0))
    n_out = 2 if with_bf16 else 1
    outs = pl.pallas_call(
        _ln_kernel,
        grid=(m // tr,),
        in_specs=[row, vec, vec],
        out_specs=[row] * n_out,
        out_shape=[jax.ShapeDtypeStruct((m, d), F32), jax.ShapeDtypeStruct((m, d), BF16)][:n_out],
        compiler_params=_cparams(("parallel",)),
        name="layer_norm",
    )(z, g.reshape(1, d), b.reshape(1, d))
    return (outs[0], outs[1]) if with_bf16 else (outs[0], None)


def _ple_kernel(xb_ref, wg_ref, p_ref, wp_ref, x_ref, of_ref, ob_ref):
    gate = jax.nn.sigmoid(jnp.dot(xb_ref[...], wg_ref[...], preferred_element_type=F32))
    emb = jnp.dot(p_ref[...].astype(BF16), wp_ref[...], preferred_element_type=F32)
    o = x_ref[...] + gate * emb
    of_ref[...] = o
    ob_ref[...] = o.astype(BF16)


def _ple(x, xb, p, wg, wp, lead):
    m, d = x.shape
    pd = p.shape[-1]
    tm = _pow2_tile(m, max(8, (8 << 20) // (2 * d)))
    tn = _lane_tile(d, max(LANES, (4 << 20) // (2 * d)))
    blk = pl.BlockSpec((tm, tn), lambda i, j: (i, j))
    return pl.pallas_call(
        _ple_kernel,
        grid=(m // tm, d // tn),
        in_specs=[pl.BlockSpec((tm, d), lambda i, j: (i, 0)), _wspec(wg, lead, d, tn),
                  pl.BlockSpec((None, tm, pd), lambda i, j: tuple(lead) + (i, 0)), _wspec(wp, lead, pd, tn), blk],
        out_specs=[blk, blk],
        out_shape=[jax.ShapeDtypeStruct((m, d), F32), jax.ShapeDtypeStruct((m, d), BF16)],
        compiler_params=_cparams(("parallel", "arbitrary")),
        name="ple",
    )(xb, wg, p, wp, x)


def _mix_kernel(x_ref, prev_ref, mu_ref, w1_ref, a1_ref, g1_ref, *rest, with_v):
    if with_v:
        v1_ref, xr_ref, xk_ref, xv_ref, tw_ref, ta_ref, tg_ref, tv_ref = rest
    else:
        xr_ref, xk_ref, xv_ref, tw_ref, ta_ref, tg_ref = rest
    x = x_ref[...]
    shifted = pltpu.roll(x, 1, 0)
    first = lax.broadcasted_iota(jnp.int32, x.shape, 0) == 0
    shifted = jnp.where(first, prev_ref[...], shifted)
    xx = shifted - x
    mix = lambda m: (x + xx * mu_ref[m:m + 1, :]).astype(BF16)
    down = lambda xm, w_ref: jnp.dot(xm, w_ref[...], preferred_element_type=F32)
    xr_ref[...] = mix(0)
    tw_ref[...] = jnp.tanh(down(mix(1), w1_ref)).astype(BF16)
    xk_ref[...] = mix(2)
    xv = mix(3)
    xv_ref[...] = xv
    ta_ref[...] = down(mix(4), a1_ref).astype(BF16)
    tg_ref[...] = jax.nn.sigmoid(down(mix(5), g1_ref)).astype(BF16)
    if with_v:
        tv_ref[...] = down(xv, v1_ref).astype(BF16)


def _token_mixes(x, x_prev, mu, w1, a1, g1, v1, layer):
    b, t, d = x.shape
    tt = _pow2_tile(t, max(8, (4 << 20) // (4 * d)))
    nt = t // tt
    prev = jnp.concatenate([x_prev[:, None, :], x[:, tt - 1:t - 1:tt, :]], axis=1).reshape(b, nt, 1, d)
    blk = pl.BlockSpec((None, tt, d), lambda i, j: (i, j, 0))
    wspec = lambda w, idx: pl.BlockSpec((None, d, w.shape[-1]), lambda i, j: (idx, 0, 0))
    tspec = lambda w: pl.BlockSpec((None, tt, w.shape[-1]), lambda i, j: (i, j, 0))
    tshape = lambda w: jax.ShapeDtypeStruct((b, t, w.shape[-1]), BF16)
    lows = [(w1, layer), (a1, layer), (g1, layer)] + ([(v1, layer - 1)] if layer > 0 else [])
    outs = pl.pallas_call(
        functools.partial(_mix_kernel, with_v=layer > 0),
        grid=(b, nt),
        in_specs=[blk,
                  pl.BlockSpec((None, None, 1, d), lambda i, j: (i, j, 0, 0)),
                  pl.BlockSpec((6, d), lambda i, j: (0, 0))] + [wspec(w, idx) for w, idx in lows],
        out_specs=[blk] * 3 + [tspec(w) for w, _ in lows],
        out_shape=[jax.ShapeDtypeStruct((b, t, d), BF16)] * 3 + [tshape(w) for w, _ in lows],
        compiler_params=_cparams(("parallel", "parallel")),
        name="token_mix",
    )(x, prev, mu, *[w for w, _ in lows])
    return [o.reshape(b * t, d) for o in outs[:3]], list(outs[3:])


def _pad_lora(w1, w2):
    r = w1.shape[-1]
    rp = -(-r // LANES) * LANES
    w1 = jnp.pad(w1.astype(BF16), ((0, 0), (0, 0), (0, rp - r)))
    w2 = jnp.pad(w2.astype(BF16), ((0, 0), (0, rp - r), (0, 0)))
    return w1, w2


def _round_robin(gens, batched=None):
    replies = [None] * len(gens)
    while gens:
        alive, asks = [], []
        for g, reply in zip(gens, replies):
            try:
                asks.append(g.send(reply))
                alive.append(g)
            except StopIteration:
                pass
        gens = alive
        replies = batched(asks) if asks and asks[0] is not None else [None] * len(gens)


def _split_bf16(x):
    hi = x.astype(BF16)
    lo = (x - hi.astype(F32)).astype(BF16)
    return hi, lo


def _wkv_kernel(*refs, chunk, n_chunks, n_slabs, with_v, t_valid):
    if with_v:
        (r_ref, k_ref, v_ref, vf_ref, tw_ref, ta_ref, tg_ref, tv_ref, w2_ref, a2_ref, g2_ref, v2_ref,
         prm_ref, s0_ref, o_ref, sf_ref, s_scr) = refs
    else:
        (r_ref, k_ref, v_ref, tw_ref, ta_ref, tg_ref, w2_ref, a2_ref, g2_ref,
         prm_ref, s0_ref, o_ref, sf_ref, s_scr) = refs
    c = chunk
    rows = HEADS_PER_SLAB * c
    tstep = pl.program_id(2)
    t_block = n_chunks * c

    lane = lax.broadcasted_iota(jnp.int32, (1, SLAB), 1)
    head_masks = [(lane // RWKV_HEAD) == j for j in range(HEADS_PER_SLAB)]
    ri = lax.broadcasted_iota(jnp.int32, (rows, rows), 0)
    ci = lax.broadcasted_iota(jnp.int32, (rows, rows), 1)
    tri_strict = (ri % c) > (ci % c)
    tri_incl_cat = (lax.broadcasted_iota(jnp.int32, (c, rows), 0)
                    >= lax.broadcasted_iota(jnp.int32, (c, rows), 1) % c)
    eye = jnp.where(ri == ci, 1.0, 0.0).astype(F32)
    rs = lax.broadcasted_iota(jnp.int32, (SLAB, SLAB), 0)
    cs = lax.broadcasted_iota(jnp.int32, (SLAB, SLAB), 1)
    same_head = (rs // RWKV_HEAD) == (cs // RWKV_HEAD)
    ones_bd = jnp.where(same_head, 1.0, 0.0).astype(BF16)
    lt_incl = jnp.where(lax.broadcasted_iota(jnp.int32, (c, c), 0) >= lax.broadcasted_iota(jnp.int32, (c, c), 1),
                        1.0, 0.0).astype(BF16)

    @pl.when(tstep == 0)
    def _():
        for q in range(n_slabs):
            s0 = s0_ref[q]
            s_scr[q] = jnp.where(same_head, jnp.concatenate([s0] * HEADS_PER_SLAB, axis=0), 0.0)

    def stack(x):
        return jnp.concatenate([jnp.where(m, x, 0.0) for m in head_masks], axis=0)

    def fold(y):
        out = y[0:c]
        for j in range(1, HEADS_PER_SLAB):
            out = out + y[j * c:(j + 1) * c]
        return out

    def skip_rows(m):
        return (min(m, c) // SUBLANES) * SUBLANES

    def head_rows(x, lo):
        if lo == 0:
            return x
        return jnp.concatenate([x[j * c + lo:(j + 1) * c] for j in range(HEADS_PER_SLAB)], axis=0)

    def head_rows_back(x, lo):
        if lo == 0:
            return x
        keep = c - lo
        zeros = jnp.zeros((lo, x.shape[1]), x.dtype)
        return jnp.concatenate([blk for j in range(HEADS_PER_SLAB)
                                for blk in (zeros, x[j * keep:(j + 1) * keep])], axis=0)

    def head_sums(asks):
        flat = [x.astype(BF16) for xs in asks for x in xs]
        tot = jnp.dot(jnp.concatenate(flat, axis=0), ones_bd, preferred_element_type=F32)
        parts = iter(tot[i * c:(i + 1) * c] for i in range(len(flat)))
        return [[next(parts) for _ in xs] for xs in asks]

    n_sq = int(math.log2(c)) - 1

    def chunk_step(q, start):
        sl = pl.ds(start, c)
        lanes = slice(q * SLAB, (q + 1) * SLAB)
        prm = prm_ref[:, lanes]
        k_k, k_a, r_k, ln_w, ln_b, w0, a0, v0 = (prm[i:i + 1, :] for i in range(8))
        up = lambda t_ref, w_ref: jnp.dot(t_ref[sl, :], w_ref[:, lanes], preferred_element_type=F32)
        r = r_ref[sl, lanes]
        k = k_ref[sl, lanes]
        v = v_ref[sl, lanes]
        u = -(w0 + up(tw_ref, w2_ref))
        e = jnp.exp(-(jnp.maximum(u, 0.0) + jnp.log(1.0 + jnp.exp(-jnp.abs(u)))) - 0.5)
        if t_valid is not None:
            frame = tstep * t_block + start + lax.broadcasted_iota(jnp.int32, (c, SLAB), 0)
            e = jnp.where(frame < t_valid, e, 0.0)
        a = jax.nn.sigmoid(a0 + up(ta_ref, a2_ref))
        g = up(tg_ref, g2_ref)
        if with_v:
            v = v + (vf_ref[sl, lanes] - v) * jax.nn.sigmoid(v0 + up(tv_ref, v2_ref))
        yield

        kk = k * k_k
        k2 = k * (1.0 + (a - 1.0) * k_a)
        ss, bonus = yield [kk * kk, r * k2 * r_k]
        kkn = kk / jnp.maximum(jnp.sqrt(ss), 1e-12)
        av = -kkn
        bv = kkn * a

        e_hi, e_lo = _split_bf16(e)
        cum = (jnp.dot(lt_incl, e_hi, preferred_element_type=F32)
               + jnp.dot(lt_incl, e_lo, preferred_element_type=F32))
        yield
        cum_last = cum[c - 1:c, :]
        p_in = jnp.exp(-cum)
        p_ex = jnp.exp(e - cum)
        p_inv = jnp.exp(cum)
        p_end = jnp.exp(cum - cum_last)
        p_c = jnp.exp(-cum_last)

        b_st = stack(bv * p_inv)
        k_st = stack(k2 * p_inv)
        v_st = stack(v)
        ar = jnp.concatenate([stack(av * p_ex), r * p_in], axis=0).astype(BF16)
        xb = _dot_nt(ar, b_st)
        yield
        xk = _dot_nt(ar, k_st)
        yield
        a_ab = jnp.where(tri_strict, xb[:rows], 0.0)
        a_rb = jnp.where(tri_incl_cat, xb[rows:], 0.0)
        a_ak = jnp.where(tri_strict, xk[:rows], 0.0)
        a_rk = jnp.where(tri_incl_cat, xk[rows:], 0.0)

        x_pow = _dot(a_ab, a_ab)
        akv = _dot(a_ak, v_st)
        yield
        t_inv = eye + a_ab
        m = 2
        for _ in range(n_sq - 1):
            lo_t, lo_x = skip_rows(m), skip_rows(2 * m)
            both = _dot(jnp.concatenate([head_rows(t_inv, lo_t), head_rows(x_pow, lo_x)], axis=0), x_pow)
            yield
            n_t = HEADS_PER_SLAB * (c - lo_t)
            t_inv = t_inv + head_rows_back(both[:n_t], lo_t)
            x_pow = head_rows_back(both[n_t:], lo_x)
            m *= 2
        lo_t = skip_rows(m)
        t_inv = t_inv + head_rows_back(_dot(head_rows(t_inv, lo_t), x_pow), lo_t)
        yield

        s_bd = s_scr[q]
        ars = _dot_nt(ar, s_bd)
        yield
        u_st = _dot(t_inv, ars[:rows] + akv)
        yield
        y = ars[rows:] + _dot(a_rb, u_st) + _dot(a_rk, v_st)
        yield
        uv = jnp.concatenate([fold(u_st), v], axis=0)
        bk = jnp.concatenate([bv * p_end, k2 * p_end], axis=0)
        s_scr[q] = s_bd * p_c + jnp.where(same_head, _dot_tn(uv, bk), 0.0)
        yield

        mean_hi, mean_lo = yield list(_split_bf16(y))
        yc = y - (mean_hi + mean_lo) * (1.0 / RWKV_HEAD)
        (var,) = yield [yc * yc]
        yn = yc * lax.rsqrt(var * (1.0 / RWKV_HEAD) + GN_EPS) * ln_w + ln_b
        o_ref[sl, lanes] = ((yn + bonus * v) * g).astype(o_ref.dtype)

    def body(idx, carry):
        start = pl.multiple_of(idx * c, c)
        _round_robin([chunk_step(q, start) for q in range(n_slabs)], batched=head_sums)
        return carry

    lax.fori_loop(0, n_chunks, body, 0)

    @pl.when(tstep == pl.num_programs(2) - 1)
    def _():
        for q in range(n_slabs):
            s_bd = s_scr[q]
            out = s_bd[0:RWKV_HEAD]
            for j in range(1, HEADS_PER_SLAB):
                out = out + s_bd[j * RWKV_HEAD:(j + 1) * RWKV_HEAD]
            sf_ref[q] = out


def _wkv(r, k, v, v_first, lows, ups, prm, layer, s0):
    b, t_in, d = r.shape
    n = RWKV_HEAD
    nslab = d // SLAB
    c = WKV_CHUNK
    t = -(-t_in // c) * c
    with_v = v_first is not None
    seqs = [r, k, v] + ([v_first] if with_v else [])
    if t != t_in:
        pad = lambda z: jnp.pad(z, ((0, 0), (0, t - t_in), (0, 0)))
        seqs, lows = [pad(z) for z in seqs], [pad(z) for z in lows]
    ns = max(s for s in (1, 2, 4, 8) if s <= WKV_SLABS_PER_STEP and nslab % s == 0)
    tb = _pow2_tile(t, max(c, WKV_BLOCK_ELEMS // (ns * SLAB)))
    s0_slab = s0.reshape(b, nslab, HEADS_PER_SLAB, n, n).transpose(0, 1, 3, 2, 4).reshape(b, nslab, n, SLAB)
    seq = pl.BlockSpec((None, tb, ns * SLAB), lambda i, s, j: (i, j, s))
    low = lambda z: pl.BlockSpec((None, tb, z.shape[-1]), lambda i, s, j: (i, j, 0))
    up = lambda w, idx: pl.BlockSpec((None, w.shape[1], ns * SLAB), lambda i, s, j: (idx, 0, s))
    st = pl.BlockSpec((None, ns, n, SLAB), lambda i, s, j: (i, s, 0, 0))
    out, sf = pl.pallas_call(
        functools.partial(_wkv_kernel, chunk=c, n_chunks=tb // c, n_slabs=ns, with_v=with_v,
                          t_valid=t_in if t != t_in else None),
        grid=(b, nslab // ns, t // tb),
        in_specs=([seq] * len(seqs) + [low(z) for z in lows] + [up(w, idx) for w, idx in ups]
                  + [pl.BlockSpec((None, 8, ns * SLAB), lambda i, s, j: (layer, 0, s)), st]),
        out_specs=[seq, st],
        out_shape=[jax.ShapeDtypeStruct((b, t, d), BF16), jax.ShapeDtypeStruct((b, nslab, n, SLAB), F32)],
        scratch_shapes=[pltpu.VMEM((ns, SLAB, SLAB), F32)],
        compiler_params=_cparams(("parallel", "parallel", "arbitrary")),
        name="wkv7",
    )(*seqs, *lows, *[w for w, _ in ups], prm, s0_slab)
    sf = sf.reshape(b, nslab, n, HEADS_PER_SLAB, n).transpose(0, 1, 3, 2, 4).reshape(b, d // n, n, n)
    return out[:, :t_in], sf


def _attn_prompt_kernel(q_ref, k_ref, v_ref, bias_ref, o_ref, kp, vp, *, t, scale, group):
    dh = kp.shape[1]
    zeros_past = jnp.zeros((BAND_PAST, dh), BF16)
    kp[0:BAND_PAST, :] = zeros_past
    vp[0:BAND_PAST, :] = zeros_past
    kp[BAND_PAST:, :] = k_ref[...].astype(BF16)
    vp[BAND_PAST:, :] = v_ref[...].astype(BF16)
    col = lax.broadcasted_iota(jnp.int32, (Q_BLK, KEY_WIN), 1)

    def block(c, mask_start):
        start = pl.multiple_of(c * Q_BLK, Q_BLK)
        s = _dot_nt(q_ref[pl.ds(start, Q_BLK), :], kp[pl.ds(start, KEY_WIN), :])
        yield
        s = s * scale + bias_ref[...]
        if mask_start:
            s = jnp.where(col >= BAND_PAST - c * Q_BLK, s, NEG_INF)
        m = jnp.max(s, axis=-1, keepdims=True)
        p = jnp.exp(s - m)
        p = p / jnp.sum(p, axis=-1, keepdims=True)
        o = _dot(p, vp[pl.ds(start, KEY_WIN), :])
        yield
        o_ref[pl.ds(start, Q_BLK), :] = o.astype(o_ref.dtype)

    def make_body(mask_start):
        def body(g, carry):
            _round_robin([block(g * group + i, mask_start) for i in range(group)])
            return carry
        return body

    n_grp = t // (Q_BLK * group)
    n_front = min(-(-BAND_PAST // (Q_BLK * group)), n_grp)
    lax.fori_loop(0, n_front, make_body(True), 0)
    lax.fori_loop(n_front, n_grp, make_body(False), 0)


def _attn_prompt(qkv, bias, b, t):
    d = qkv.shape[1] // 3
    h = d // ATTN_HEAD_DIM
    assert t % Q_BLK == 0
    group = max(g for g in (1, 2, 4, 8, 16) if g <= ATTN_GROUP and (t // Q_BLK) % g == 0)
    col = lambda off: pl.BlockSpec((t, ATTN_HEAD_DIM), lambda i, j: (i, off + j))
    return pl.pallas_call(
        functools.partial(_attn_prompt_kernel, t=t, scale=ATTN_HEAD_DIM ** -0.5, group=group),
        grid=(b, h),
        in_specs=[col(0), col(h), col(2 * h),
                  pl.BlockSpec((None, Q_BLK, KEY_WIN), lambda i, j: (j, 0, 0))],
        out_specs=col(0),
        out_shape=jax.ShapeDtypeStruct((b * t, d), BF16),
        scratch_shapes=[pltpu.VMEM((BAND_PAST + t, ATTN_HEAD_DIM), BF16)] * 2,
        compiler_params=_cparams(("parallel", "parallel")),
        name="band_attn_prompt",
    )(qkv, qkv, qkv, bias)


def _attn_sample_kernel(q_ref, kn_ref, vn_ref, kc_ref, vc_ref, bias_ref, o_ref, kf, vf, *, rows, heads, scale):
    t = q_ref.shape[0]
    dh = ATTN_HEAD_DIM
    group = kf.shape[0]
    pad = jnp.zeros((kf.shape[1] - rows - t, dh), BF16)

    def head(h, slot):
        ln = slice(h * dh, (h + 1) * dh)
        for buf, cache_ref, new_ref in ((kf, kc_ref, kn_ref), (vf, vc_ref, vn_ref)):
            buf[slot, 0:rows, :] = cache_ref[pl.ds(h, rows, stride=heads), :].astype(BF16)
            buf[slot, rows:rows + t, :] = new_ref[:, ln].astype(BF16)
            buf[slot, rows + t:, :] = pad
        s = _dot_nt(q_ref[:, ln], kf[slot])
        yield
        s = s * scale + bias_ref[h]
        col = lax.broadcasted_iota(jnp.int32, s.shape, 1)
        s = jnp.where(col < rows + t, s, NEG_INF)
        m = jnp.max(s, axis=-1, keepdims=True)
        p = jnp.exp(s - m)
        p = p / jnp.sum(p, axis=-1, keepdims=True)
        o = _dot(p, vf[slot])
        yield
        o_ref[:, ln] = o.astype(o_ref.dtype)

    for h0 in range(0, heads, group):
        _round_robin([head(h0 + i, i) for i in range(group)])


def _attn_sample(qkv, cache_k, cache_v, layer, bias, b, t):
    d = qkv.shape[1] // 3
    h = d // ATTN_HEAD_DIM
    rows = cache_k.shape[2]
    win = bias.shape[2]
    group = max(s for s in (1, 2, 4) if h % s == 0)
    flat = lambda c: c.reshape(c.shape[0], c.shape[1], rows * h, ATTN_HEAD_DIM)
    col = lambda off: pl.BlockSpec((t, d), lambda i: (i, off))
    cache = pl.BlockSpec((None, None, rows * h, ATTN_HEAD_DIM), lambda i: (i, layer, 0, 0))
    return pl.pallas_call(
        functools.partial(_attn_sample_kernel, rows=rows, heads=h, scale=ATTN_HEAD_DIM ** -0.5),
        grid=(b,),
        in_specs=[col(0), col(1), col(2), cache, cache, pl.BlockSpec((h, t, win), lambda i: (0, 0, 0))],
        out_specs=col(0),
        out_shape=jax.ShapeDtypeStruct((b * t, d), BF16),
        scratch_shapes=[pltpu.VMEM((group, win, ATTN_HEAD_DIM), BF16)] * 2,
        compiler_params=_cparams(("parallel",)),
        name="band_attn_sample",
    )(qkv, qkv, qkv, flat(cache_k), flat(cache_v), bias)


def _rel_bias(table, n_q, n_k, offset):
    span = n_q + n_k - 1
    dist = (n_q - 1) - jnp.arange(span) + offset
    rev = table[:, jnp.clip(dist, -(CHUNK - 1), REL_MAX) + (CHUNK - 1)].astype(F32)
    period = jnp.pad(rev, ((0, 0), (0, 1)))
    shifted = jnp.tile(period, (1, n_q))[:, :n_q * span].reshape(table.shape[0], n_q, span)
    return shifted[:, :, n_q - 1:n_q - 1 + n_k]


def _trunk(x, p, shift0, wkv0, cache_k, cache_v, w, is_prompt):
    b, t, d = x.shape
    m = b * t
    depth = w["ln_g"].shape[0]
    alpha = (2 * depth) ** 0.25
    xf = x.reshape(m, d)
    xb = xf.astype(BF16)
    p = p.reshape(depth, m, p.shape[-1])
    v_first = None
    shifts, wkvs, ks, vs = [], [], [], []

    def ffn(xf, xb, i, s, ln_idx, with_bf16=True):
        hidden = _ffn_up(xb, w["ffn_gate"], w["ffn_up"], (i, s))
        z = _matmul(hidden, w["ffn_down"], (i, s), F32, "ffn_down", residual=(xf, alpha, 0.5))
        return _layer_norm(z, w["ln_g"][i, ln_idx], w["ln_b"][i, ln_idx], with_bf16)

    for i in range(depth):
        xf, xb = ffn(xf, xb, i, 0, 0, with_bf16=(i % 2 == 1))
        j = i // 2
        if i % 2 == 0:
            x3 = xf.reshape(b, t, d)
            (w1, w2), (a1, a2), (g1, g2), (v1, v2) = (w[n] for n in ("rwkv_w12", "rwkv_a12", "rwkv_g12", "rwkv_v12"))
            (xr, xk, xv), lows = _token_mixes(x3, shift0[:, j], w["rwkv_mu"][j], w1, a1, g1, v1, j)
            shifts.append(x3[:, -1])
            to3 = lambda z: z.reshape(b, t, d)
            r = to3(_matmul(xr, w["rwkv_w_r"], (j,), F32, "rwkv_r"))
            k = to3(_matmul(xk, w["rwkv_w_k"], (j,), F32, "rwkv_k"))
            v = to3(_matmul(xv, w["rwkv_w_v"], (j,), F32, "rwkv_v"))
            ups = [(w2, j), (a2, j), (g2, j)] + ([(v2, j - 1)] if j > 0 else [])
            mix, s_fin = _wkv(r, k, v, v_first if j > 0 else None, lows, ups, w["rwkv_prm"], j, wkv0[:, j])
            if j == 0:
                v_first = v
            wkvs.append(s_fin)
            z = _matmul(mix.reshape(m, d), w["rwkv_w_o"], (j,), F32, "rwkv_o", residual=(xf, alpha, 1.0))
        else:
            qkv = _matmul(xb, w["attn_w_qkv"], (j,), F32, "attn_qkv")
            kv3 = qkv.reshape(b, t, 3 * d)
            h = d // ATTN_HEAD_DIM
            if is_prompt:
                rows = min(BAND_PAST, t)
                qi = jnp.arange(Q_BLK)[:, None]
                kj = jnp.arange(KEY_WIN)[None, :]
                in_band = jnp.where(qi < CHUNK, kj < BAND_PAST + CHUNK, kj >= CHUNK)
                bias = jnp.where(in_band[None], _rel_bias(w["attn_rel_bias"][j], Q_BLK, KEY_WIN, BAND_PAST), NEG_INF)
                att = _attn_prompt(qkv, bias, b, t)
                ks.append(kv3[:, t - rows:, d:2 * d].reshape(b, rows, h, ATTN_HEAD_DIM))
                vs.append(kv3[:, t - rows:, 2 * d:].reshape(b, rows, h, ATTN_HEAD_DIM))
            else:
                rows = cache_k.shape[2]
                win = -(-(rows + t) // LANES) * LANES
                att = _attn_sample(qkv, cache_k, cache_v, j, _rel_bias(w["attn_rel_bias"][j], t, win, rows), b, t)
                ks.append(kv3[:, :, d:2 * d].reshape(b, t, h, ATTN_HEAD_DIM))
                vs.append(kv3[:, :, 2 * d:].reshape(b, t, h, ATTN_HEAD_DIM))
            z = _matmul(att, w["attn_w_o"], (j,), F32, "attn_o", residual=(xf, alpha, 1.0))
        xf, xb = _layer_norm(z, w["ln_g"][i, 1], w["ln_b"][i, 1])
        xf, xb = ffn(xf, xb, i, 1, 2)
        xf, xb = _ple(xf, xb, p, w["ple_w_gate"], w["ple_w_proj"], (i,))
    return (xf.reshape(b, t, d), jnp.stack(shifts, axis=1), jnp.stack(wkvs, axis=1),
            jnp.stack(ks, axis=1), jnp.stack(vs, axis=1))


def _prepare_weights(ln_g, ln_b, ffn_w_gate, ffn_w_up, ffn_w_down, ple_w_gate, ple_w_proj,
                     rwkv_mu, rwkv_w_r, rwkv_w_k, rwkv_w_v, rwkv_w_o, rwkv_w0, rwkv_w1, rwkv_w2,
                     rwkv_a0, rwkv_a1, rwkv_a2, rwkv_v0, rwkv_v1, rwkv_v2, rwkv_g1, rwkv_g2,
                     rwkv_k_k, rwkv_k_a, rwkv_r_k, rwkv_ln_w, rwkv_ln_b,
                     attn_w_qkv, attn_w_o, attn_rel_bias):
    n_rwkv = rwkv_mu.shape[0]
    d = ln_g.shape[-1]
    cast = lambda a: a.astype(BF16)
    v0 = jnp.concatenate([jnp.zeros((1, d), F32), rwkv_v0], axis=0)
    return dict(
        ln_g=ln_g, ln_b=ln_b,
        ffn_gate=cast(ffn_w_gate), ffn_up=cast(ffn_w_up), ffn_down=cast(ffn_w_down),
        ple_w_gate=cast(ple_w_gate), ple_w_proj=cast(ple_w_proj),
        rwkv_mu=rwkv_mu,
        rwkv_w_r=cast(rwkv_w_r), rwkv_w_k=cast(rwkv_w_k), rwkv_w_v=cast(rwkv_w_v), rwkv_w_o=cast(rwkv_w_o),
        rwkv_w12=_pad_lora(rwkv_w1, rwkv_w2), rwkv_a12=_pad_lora(rwkv_a1, rwkv_a2),
        rwkv_v12=_pad_lora(rwkv_v1, rwkv_v2), rwkv_g12=_pad_lora(rwkv_g1, rwkv_g2),
        rwkv_prm=jnp.stack([rwkv_k_k, rwkv_k_a, rwkv_r_k.reshape(n_rwkv, d), rwkv_ln_w, rwkv_ln_b,
                            rwkv_w0, rwkv_a0, v0], axis=1),
        attn_w_qkv=cast(attn_w_qkv), attn_w_o=cast(attn_w_o), attn_rel_bias=attn_rel_bias,
    )


def kernel(x_prompt, x_sample, state_shift, state_wkv, cache_k, cache_v, p_prompt, p_sample,
           ln_g, ln_b, ffn_w_gate, ffn_w_up, ffn_w_down, ple_w_gate, ple_w_proj,
           rwkv_mu, rwkv_w_r, rwkv_w_k, rwkv_w_v, rwkv_w_o, rwkv_w0, rwkv_w1, rwkv_w2,
           rwkv_a0, rwkv_a1, rwkv_a2, rwkv_v0, rwkv_v1, rwkv_v2, rwkv_g1, rwkv_g2,
           rwkv_k_k, rwkv_k_a, rwkv_r_k, rwkv_ln_w, rwkv_ln_b,
           attn_w_qkv, attn_w_o, attn_rel_bias):
    w = _prepare_weights(ln_g, ln_b, ffn_w_gate, ffn_w_up, ffn_w_down, ple_w_gate, ple_w_proj,
                         rwkv_mu, rwkv_w_r, rwkv_w_k, rwkv_w_v, rwkv_w_o, rwkv_w0, rwkv_w1, rwkv_w2,
                         rwkv_a0, rwkv_a1, rwkv_a2, rwkv_v0, rwkv_v1, rwkv_v2, rwkv_g1, rwkv_g2,
                         rwkv_k_k, rwkv_k_a, rwkv_r_k, rwkv_ln_w, rwkv_ln_b,
                         attn_w_qkv, attn_w_o, attn_rel_bias)
    bp, _, d = x_prompt.shape
    n_rwkv = rwkv_mu.shape[0]
    heads = d // RWKV_HEAD
    zero_shift = jnp.zeros((bp, n_rwkv, d), x_prompt.dtype)
    zero_wkv = jnp.zeros((bp, n_rwkv, heads, RWKV_HEAD, RWKV_HEAD), x_prompt.dtype)
    y_p, shift_p, wkv_p, k_p, v_p = _trunk(x_prompt, p_prompt, zero_shift, zero_wkv, None, None, w, True)
    y_s, shift_s, wkv_s, k_s, v_s = _trunk(x_sample, p_sample, state_shift, state_wkv, cache_k, cache_v, w, False)
    return (y_p, y_s, shift_p, wkv_p, k_p, v_p, shift_s, wkv_s, k_s, v_s)
```
